```python
import math
import jax, jax.numpy as jnp
from jax import lax
import numpy as np

D_MODEL = 2048
BATCH = 4
SEQ = 2048
DEPTH = 2

N_A = DEPTH // 2
N_B = DEPTH - N_A

M_HEADS = 8
M_QK_DIM = D_MODEL // 2 // M_HEADS
M_V_DIM = D_MODEL // M_HEADS
M_CHUNK = 64
GATE_SOFTCAP = 15.0
M_HQK = M_HEADS * M_QK_DIM
M_HV = M_HEADS * M_V_DIM
M_SPLITS = (M_HQK, 2 * M_HQK, 2 * M_HQK + M_HV, 2 * M_HQK + 2 * M_HV, 2 * M_HQK + 2 * M_HV + M_HEADS)
M_IN_COLS = 2 * M_HQK + 2 * M_HV + 2 * M_HEADS

A_HEADS = 8
A_QK_DIM = D_MODEL // (2 * A_HEADS)
A_V_DIM = 2 * A_QK_DIM
A_HQK = A_HEADS * 2 * A_QK_DIM
A_HV = A_HEADS * A_V_DIM
ROPE_DIM = A_QK_DIM // 4
ROPE_THETA = 500000.0
Q_BLOCK = 128

D_FF = ((8 * D_MODEL // 3 + 255) // 256) * 256
CONV_W = 3
EPS = 1e-6

kernel_name = "yoco_mlstm_diffattn_convffn"


def rms_norm(x, g):
    xf = x.astype(jnp.float32)
    y = xf * lax.rsqrt(jnp.mean(xf * xf, axis=-1, keepdims=True) + EPS)
    return (y * g.astype(jnp.float32)).astype(x.dtype)


def softcap(t):
    return GATE_SOFTCAP * jnp.tanh(t / GATE_SOFTCAP)


def mlstm_chunkwise(q, k, v, i_pre, log_f):
    B, H, S, Dk = q.shape
    Dv = v.shape[-1]
    L = M_CHUNK
    nc = S // L
    q = q * (Dk ** -0.5)

    def to_chunks(t):
        return jnp.moveaxis(t.reshape((B, H, nc, L) + t.shape[3:]), 2, 0)

    qc, kc, vc, ic, fc = (to_chunks(t) for t in (q, k, v, i_pre, log_f))
    causal = jnp.tril(jnp.ones((L, L), dtype=bool))

    def step(carry, inp):
        C, n, m = carry
        qb, kb, vb, ib, fb = inp
        b = jnp.cumsum(fb, axis=-1)
        dmat = b[..., :, None] - b[..., None, :] + ib[..., None, :]
        dmat = jnp.where(causal, dmat, -jnp.inf)
        inter = b + m[..., None]
        m_t = jnp.maximum(inter, jnp.max(dmat, axis=-1))
        w_inter = jnp.exp(inter - m_t)
        s = jnp.einsum('bhtd,bhsd->bhts', qb, kb) * jnp.exp(dmat - m_t[..., None])
        num = w_inter[..., None] * jnp.einsum('bhvd,bhtd->bhtv', C, qb) + jnp.einsum('bhts,bhsv->bhtv', s, vb)
        den = w_inter * jnp.einsum('bhd,bhtd->bht', n, qb) + jnp.sum(s, axis=-1)
        h = num / jnp.maximum(jnp.abs(den), jnp.exp(-m_t))[..., None]
        b_last = b[..., -1]
        g = b_last[..., None] - b + ib
        m_new = jnp.maximum(b_last + m, jnp.max(g, axis=-1))
        decay = jnp.exp(b_last + m - m_new)
        wg = jnp.exp(g - m_new[..., None])
        C_new = decay[..., None, None] * C + jnp.einsum('bhs,bhsv,bhsd->bhvd', wg, vb, kb)
        n_new = decay[..., None] * n + jnp.einsum('bhs,bhsd->bhd', wg, kb)
        return (C_new, n_new, m_new), h

    init = (jnp.zeros((B, H, Dv, Dk), jnp.float32),
            jnp.zeros((B, H, Dk), jnp.float32),
            jnp.zeros((B, H), jnp.float32))
    _, hc = lax.scan(step, init, (qc, kc, vc, ic, fc))
    return jnp.moveaxis(hc, 0, 2).reshape(B, H, S, Dv)


def mlstm_mixer(xn, w_in, b_igate, b_fgate, w_hnorm, w_out):
    B, S, _ = xn.shape
    proj = xn @ w_in
    q, k, v, o, ig, fg = jnp.split(proj, M_SPLITS, axis=-1)

    def heads(t, d):
        return t.reshape(B, S, M_HEADS, d).transpose(0, 2, 1, 3).astype(jnp.float32)

    q, k, v = heads(q, M_QK_DIM), heads(k, M_QK_DIM), heads(v, M_V_DIM)
    i_pre = softcap((ig + b_igate).astype(jnp.float32)).transpose(0, 2, 1)
    log_f = jax.nn.log_sigmoid(softcap((fg + b_fgate).astype(jnp.float32))).transpose(0, 2, 1)
    h = mlstm_chunkwise(q, k, v, i_pre, log_f)
    h = rms_norm(h.transpose(0, 2, 1, 3), w_hnorm).astype(xn.dtype)
    h = h.reshape(B, S, M_HV) * jax.nn.sigmoid(o)
    return h @ w_out


def rope_partial(x, pos):
    half = ROPE_DIM // 2
    inv_freq = ROPE_THETA ** (-jnp.arange(half, dtype=jnp.float32) / half)
    ang = pos.astype(jnp.float32)[..., None] * inv_freq
    cos = jnp.cos(ang)[:, :, None, :]
    sin = jnp.sin(ang)[:, :, None, :]
    x1 = x[..., :half].astype(jnp.float32)
    x2 = x[..., half:ROPE_DIM].astype(jnp.float32)
    rot = jnp.concatenate([x1 * cos - x2 * sin, x2 * cos + x1 * sin], axis=-1).astype(x.dtype)
    return jnp.concatenate([rot, x[..., ROPE_DIM:]], axis=-1)


def shared_kv(h, g_kv, w_kv, pos):
    B, S, _ = h.shape
    kv = rms_norm(h, g_kv) @ w_kv
    k, v = jnp.split(kv, (A_HQK,), axis=-1)
    k = rope_partial(k.reshape(B, S, 2 * A_HEADS, A_QK_DIM), pos).reshape(B, S, A_HEADS, 2, A_QK_DIM)
    v = v.reshape(B, S, A_HEADS, A_V_DIM)
    return k, v


def diff_attention(xn, k, v, pos, w_q, lam_q1, lam_k1, lam_q2, lam_k2, g_subln, w_o, lambda_init):
    B, S, _ = xn.shape
    q = rope_partial((xn @ w_q).reshape(B, S, 2 * A_HEADS, A_QK_DIM), pos)
    q = q.reshape(B, S, A_HEADS, 2, A_QK_DIM) * (A_QK_DIM ** -0.5)
    lam = (jnp.exp(jnp.sum(lam_q1.astype(jnp.float32) * lam_k1.astype(jnp.float32)))
           - jnp.exp(jnp.sum(lam_q2.astype(jnp.float32) * lam_k2.astype(jnp.float32))) + lambda_init)
    outs = []
    for start in range(0, S, Q_BLOCK):
        end = start + Q_BLOCK
        qb, kb, vb = q[:, start:end], k[:, :end], v[:, :end]
        s = jnp.einsum('bqhcd,bkhcd->bhcqk', qb, kb).astype(jnp.float32)
        mask = (start + jnp.arange(Q_BLOCK))[:, None] >= jnp.arange(end)[None, :]
        p = jax.nn.softmax(jnp.where(mask, s, -jnp.inf), axis=-1)
        pd = p[:, :, 0] - lam * p[:, :, 1]
        outs.append(jnp.einsum('bhqk,bkhv->bqhv', pd.astype(vb.dtype), vb))
    o = jnp.concatenate(outs, axis=1)
    o = rms_norm(o, g_subln) * (1.0 - lambda_init)
    return o.reshape(B, S, A_HV) @ w_o


def conv_ffn(xn, w_up, conv_w, conv_b, w_down):
    S = xn.shape[1]
    u = xn @ w_up
    up = jnp.pad(u, ((0, 0), (CONV_W - 1, 0), (0, 0)))
    c = conv_b + up[:, 0:S] * conv_w[0]
    for j in range(1, CONV_W):
        c = c + up[:, j:j + S] * conv_w[j]
    gate, val = jnp.split(c, 2, axis=-1)
    return (jax.nn.silu(gate) * val) @ w_down


def setup_inputs(seed: int = 0) -> dict:
    key = jax.random.key(seed)
    ks = jax.random.split(key, 32)
    f32 = jnp.float32
    nrm = lambda k, shape, scale: jax.random.normal(k, shape, f32) * scale
    gain = lambda k, shape: 1.0 + 0.05 * jax.random.normal(k, shape, f32)
    x = jax.random.normal(ks[0], (BATCH, SEQ, D_MODEL), f32)
    offset = jax.random.randint(ks[1], (BATCH, 1), 0, 4096, dtype=jnp.int32)
    positions = offset + jnp.arange(SEQ, dtype=jnp.int32)[None, :]
    fbias = jnp.linspace(3.0, 6.0, M_HEADS, dtype=f32)[None, :] + 0.1 * jax.random.normal(ks[5], (N_A, M_HEADS), f32)
    return {
        "x": x,
        "positions": positions,
        "a_norm": gain(ks[2], (N_A, D_MODEL)),
        "m_w_in": nrm(ks[3], (N_A, D_MODEL, M_IN_COLS), D_MODEL ** -0.5),
        "m_b_igate": nrm(ks[4], (N_A, M_HEADS), 0.1),
        "m_b_fgate": fbias,
        "m_w_hnorm": gain(ks[6], (N_A, M_HEADS, M_V_DIM)),
        "m_w_out": nrm(ks[7], (N_A, M_HV, D_MODEL), M_HV ** -0.5),
        "kv_norm": gain(ks[8], (D_MODEL,)),
        "w_kv": nrm(ks[9], (D_MODEL, A_HQK + A_HV), D_MODEL ** -0.5),
        "b_norm": gain(ks[10], (N_B, D_MODEL)),
        "w_q": nrm(ks[11], (N_B, D_MODEL, A_HQK), D_MODEL ** -0.5),
        "lam_q1": nrm(ks[12], (N_B, A_QK_DIM), 0.1),
        "lam_k1": nrm(ks[13], (N_B, A_QK_DIM), 0.1),
        "lam_q2": nrm(ks[14], (N_B, A_QK_DIM), 0.1),
        "lam_k2": nrm(ks[15], (N_B, A_QK_DIM), 0.1),
        "subln": gain(ks[16], (N_B, A_V_DIM)),
        "w_o": nrm(ks[17], (N_B, A_HV, D_MODEL), A_HV ** -0.5),
        "f_norm": gain(ks[18], (DEPTH, D_MODEL)),
        "w_up": nrm(ks[19], (DEPTH, D_MODEL, 2 * D_FF), D_MODEL ** -0.5),
        "conv_w": nrm(ks[20], (DEPTH, CONV_W, 2 * D_FF), CONV_W ** -0.5),
        "conv_b": nrm(ks[21], (DEPTH, 2 * D_FF), 0.01),
        "w_down": nrm(ks[22], (DEPTH, D_FF, D_MODEL), D_FF ** -0.5),
        "final_norm": gain(ks[23], (D_MODEL,)),
    }


def reference(x, positions, a_norm, m_w_in, m_b_igate, m_b_fgate, m_w_hnorm, m_w_out,
              kv_norm, w_kv, b_norm, w_q, lam_q1, lam_k1, lam_q2, lam_k2, subln, w_o,
              f_norm, w_up, conv_w, conv_b, w_down, final_norm):
    h = x
    k_sh = None
    v_sh = None
    for layer in range(DEPTH):
        if layer < N_A:
            h = h + mlstm_mixer(rms_norm(h, a_norm[layer]), m_w_in[layer], m_b_igate[layer],
                                m_b_fgate[layer], m_w_hnorm[layer], m_w_out[layer])
        else:
            j = layer - N_A
            if j == 0:
                k_sh, v_sh = shared_kv(h, kv_norm, w_kv, positions)
            lambda_init = 0.8 - 0.6 * math.exp(-0.3 * layer)
            h = h + diff_attention(rms_norm(h, b_norm[j]), k_sh, v_sh, positions, w_q[j],
                                   lam_q1[j], lam_k1[j], lam_q2[j], lam_k2[j], subln[j], w_o[j],
                                   lambda_init)
        h = h + conv_ffn(rms_norm(h, f_norm[layer]), w_up[layer], conv_w[layer], conv_b[layer], w_down[layer])
    return rms_norm(h, final_norm)
```

```python
import functools
import math

import jax
import jax.numpy as jnp
from jax import lax
from jax.experimental import pallas as pl
from jax.experimental.pallas import tpu as pltpu

F32 = jnp.float32
BF16 = jnp.bfloat16

EPS = 1e-6
M_HEADS = 8
M_QK_DIM = 128
M_V_DIM = 256
M_CHUNK = 64
GATE_SOFTCAP = 15.0
A_HEADS = 8
A_QK_DIM = 128
A_V_DIM = 256
ROPE_DIM = 32
ROPE_THETA = 500000.0
CONV_W = 3

LANES = 128
BF16_SUBLANES = 16
V7X_VMEM_LIMIT_BYTES = 56 * 1024 * 1024

PROJ_TM = 1024
W_IN_TN = 896
KVQ_TN = 512
RES_TN = 512
FFN_TM = 512
FFN_TF = 512
FFN_HALO = BF16_SUBLANES
ATT_TQ = 256


def _rms_scale(x):
    return x * lax.rsqrt(jnp.mean(x * x, axis=-1, keepdims=True) + EPS)


def _params(semantics):
    return pltpu.CompilerParams(dimension_semantics=semantics, vmem_limit_bytes=V7X_VMEM_LIMIT_BYTES)


def _norm_matmul_kernel(x_ref, g_ref, w_ref, o_ref, xn_ref):
    @pl.when(pl.program_id(1) == 0)
    def _():
        xn_ref[...] = (_rms_scale(x_ref[...]) * g_ref[...]).astype(BF16)

    o_ref[...] = jnp.dot(xn_ref[...], w_ref[...], preferred_element_type=F32)


def _norm_matmul(x, g, w, tm, tn):
    t, d = x.shape
    n = w.shape[1]
    return pl.pallas_call(
        _norm_matmul_kernel,
        grid=(t // tm, n // tn),
        in_specs=[
            pl.BlockSpec((tm, d), lambda i, j: (i, 0)),
            pl.BlockSpec((1, d), lambda i, j: (0, 0)),
            pl.BlockSpec((d, tn), lambda i, j: (0, j)),
        ],
        out_specs=pl.BlockSpec((tm, tn), lambda i, j: (i, j)),
        out_shape=jax.ShapeDtypeStruct((t, n), F32),
        scratch_shapes=[pltpu.VMEM((tm, d), BF16)],
        compiler_params=_params(("parallel", "arbitrary")),
        name="norm_w_in",
    )(x, g, w)


def _softcap(t):
    return GATE_SOFTCAP * jnp.tanh(t / GATE_SOFTCAP)


def _log_sigmoid(x):
    return jnp.minimum(x, 0.0) - jnp.log1p(jnp.exp(-jnp.abs(x)))


def _mlstm_kernel(q_ref, k_ref, v_ref, o_ref, gr_ref, gc_ref, br_ref, bc_ref, wn_ref, out_ref, *, seq):
    L = M_CHUNK
    nc = seq // L
    scale = M_QK_DIM ** -0.5
    row_t = lax.broadcasted_iota(jnp.int32, (L, L), 0)
    col_s = lax.broadcasted_iota(jnp.int32, (L, L), 1)
    causal = row_t >= col_s
    bias_r = br_ref[...]
    bias_c = bc_ref[...]
    wn = wn_ref[...]

    def chunk(c, carry):
        ct, n_row, m = carry
        r0 = pl.multiple_of(c * L, L)
        q = q_ref[pl.ds(r0, L), :] * scale
        k = k_ref[pl.ds(r0, L), :]
        v = v_ref[pl.ds(r0, L), :]
        qb = q.astype(BF16)
        kb = k.astype(BF16)
        vb = v.astype(BF16)

        i_r = _softcap(gr_ref[0, pl.ds(c, 1), :] + bias_r[0:1, :])
        f_r = _log_sigmoid(_softcap(gr_ref[1, pl.ds(c, 1), :] + bias_r[1:2, :]))
        g_c = gc_ref[c] + bias_c
        i_c = _softcap(g_c[:, 0:1])
        f_c = _log_sigmoid(_softcap(g_c[:, 1:2]))

        b_c = jnp.sum(jnp.where(causal, f_r, 0.0), axis=1, keepdims=True)
        b_r = jnp.sum(jnp.where(row_t <= col_s, f_c, 0.0), axis=0, keepdims=True)
        b_last = jnp.sum(f_r, axis=1, keepdims=True)

        dmat = jnp.where(causal, b_c - b_r + i_r, -jnp.inf)
        inter = b_c + m
        m_t = jnp.maximum(inter, jnp.max(dmat, axis=1, keepdims=True))
        w_inter = jnp.exp(inter - m_t)
        s = lax.dot_general(qb, kb, (((1,), (1,)), ((), ())), preferred_element_type=F32)
        s = s * jnp.exp(dmat - m_t)
        num = w_inter * jnp.dot(qb, ct.astype(BF16), preferred_element_type=F32)
        num = num + jnp.dot(s.astype(BF16), vb, preferred_element_type=F32)
        den = w_inter * jnp.sum(q * n_row, axis=1, keepdims=True) + jnp.sum(s, axis=1, keepdims=True)
        h = num / jnp.maximum(jnp.abs(den), jnp.exp(-m_t))

        hn = _rms_scale(h) * wn
        out_ref[pl.ds(r0, L), :] = (hn * jax.nn.sigmoid(o_ref[pl.ds(r0, L), :])).astype(out_ref.dtype)

        g = b_last - b_c + i_c
        m_new = jnp.maximum(b_last + m, jnp.max(g, axis=0, keepdims=True))
        decay = jnp.exp(b_last + m - m_new)
        kw = k * jnp.exp(g - m_new)
        ct_new = decay * ct + jnp.dot(kw.T.astype(BF16), vb, preferred_element_type=F32)
        n_new = decay * n_row + jnp.sum(kw, axis=0, keepdims=True)
        return ct_new, n_new, m_new

    init = (jnp.zeros((M_QK_DIM, M_V_DIM), F32), jnp.zeros((1, M_QK_DIM), F32), jnp.zeros((1, 1), F32))
    lax.fori_loop(0, nc, chunk, init)


def _mlstm(proj, gates_r, gates_c, bias_r, bias_c, w_hnorm, batch, seq):
    nc = seq // M_CHUNK
    qk_blocks = M_HEADS * M_QK_DIM // M_QK_DIM
    v_off = 2 * M_HEADS * M_QK_DIM // M_V_DIM
    o_off = v_off + M_HEADS
    return pl.pallas_call(
        functools.partial(_mlstm_kernel, seq=seq),
        grid=(batch, M_HEADS),
        in_specs=[
            pl.BlockSpec((seq, M_QK_DIM), lambda b, h: (b, h)),
            pl.BlockSpec((seq, M_QK_DIM), lambda b, h: (b, qk_blocks + h)),
            pl.BlockSpec((seq, M_V_DIM), lambda b, h: (b, v_off + h)),
            pl.BlockSpec((seq, M_V_DIM), lambda b, h: (b, o_off + h)),
            pl.BlockSpec((None, None, 2, nc, M_CHUNK), lambda b, h: (b, h, 0, 0, 0)),
            pl.BlockSpec((None, None, nc, M_CHUNK, 2), lambda b, h: (b, h, 0, 0, 0)),
            pl.BlockSpec((None, 2, 1), lambda b, h: (h, 0, 0)),
            pl.BlockSpec((None, 1, 2), lambda b, h: (h, 0, 0)),
            pl.BlockSpec((None, 1, M_V_DIM), lambda b, h: (h, 0, 0)),
        ],
        out_specs=pl.BlockSpec((seq, M_V_DIM), lambda b, h: (b, h)),
        out_shape=jax.ShapeDtypeStruct((batch * seq, M_HEADS * M_V_DIM), BF16),
        compiler_params=_params(("parallel", "parallel")),
        name="mlstm_chunkwise",
    )(proj, proj, proj, proj, gates_r, gates_c, bias_r, bias_c, w_hnorm)


def _matmul_residual_kernel(a_ref, w_ref, r_ref, o_ref):
    o_ref[...] = r_ref[...] + jnp.dot(a_ref[...], w_ref[...], preferred_element_type=F32)


def _matmul_residual(a, w, res, tm, tn, name):
    t, kdim = a.shape
    n = w.shape[1]
    return pl.pallas_call(
        _matmul_residual_kernel,
        grid=(t // tm, n // tn),
        in_specs=[
            pl.BlockSpec((tm, kdim), lambda i, j: (i, 0)),
            pl.BlockSpec((kdim, tn), lambda i, j: (0, j)),
            pl.BlockSpec((tm, tn), lambda i, j: (i, j)),
        ],
        out_specs=pl.BlockSpec((tm, tn), lambda i, j: (i, j)),
        out_shape=jax.ShapeDtypeStruct((t, n), F32),
        compiler_params=_params(("parallel", "parallel")),
        name=name,
    )(a, w, res)


def _conv_ffn_kernel(h_ref, hp_ref, g_ref, wg_ref, wv_ref, cwg_ref, cwv_ref, cbg_ref, cbv_ref, wd_ref,
                     gf_ref, o_ref, xn_ref, *, tiles_per_seq, final_norm):
    i = pl.program_id(0)
    f = pl.program_id(1)
    tm = h_ref.shape[0]

    @pl.when(f == 0)
    def _():
        g = g_ref[...]
        x = h_ref[...]
        xn_ref[FFN_HALO:, :] = (_rms_scale(x) * g).astype(BF16)
        keep = (i % tiles_per_seq != 0).astype(F32)
        xn_ref[:FFN_HALO, :] = (_rms_scale(hp_ref[...]) * g * keep).astype(BF16)
        o_ref[...] = x

    xn = xn_ref[...]

    def branch(w_ref, cw_ref, cb_ref):
        u = jnp.dot(xn, w_ref[...], preferred_element_type=F32)
        cw = cw_ref[...]
        c = cb_ref[...] + pltpu.roll(u, 2, 0) * cw[0:1, :] + pltpu.roll(u, 1, 0) * cw[1:2, :] + u * cw[2:3, :]
        return c[FFN_HALO:, :]

    gate = branch(wg_ref, cwg_ref, cbg_ref)
    val = branch(wv_ref, cwv_ref, cbv_ref)
    act = (gate * jax.nn.sigmoid(gate) * val).astype(BF16)
    o_ref[...] += jnp.dot(act, wd_ref[...], preferred_element_type=F32)

    if final_norm:
        @pl.when(f == pl.num_programs(1) - 1)
        def _():
            o_ref[...] = _rms_scale(o_ref[...]) * gf_ref[...]


def _conv_ffn(h, g, w_up, conv_w, conv_b, w_down, g_final, seq, final_norm, name):
    t, d = h.shape
    d_ff = w_down.shape[0]
    tm, tf = FFN_TM, FFN_TF
    nf = d_ff // tf
    halo_blocks = tm // FFN_HALO
    kern = functools.partial(_conv_ffn_kernel, tiles_per_seq=seq // tm, final_norm=final_norm)
    return pl.pallas_call(
        kern,
        grid=(t // tm, nf),
        in_specs=[
            pl.BlockSpec((tm, d), lambda i, f: (i, 0)),
            pl.BlockSpec((FFN_HALO, d), lambda i, f: (jnp.maximum(i * halo_blocks - 1, 0), 0)),
            pl.BlockSpec((1, d), lambda i, f: (0, 0)),
            pl.BlockSpec((d, tf), lambda i, f: (0, f)),
            pl.BlockSpec((d, tf), lambda i, f: (0, nf + f)),
            pl.BlockSpec((CONV_W, tf), lambda i, f: (0, f)),
            pl.BlockSpec((CONV_W, tf), lambda i, f: (0, nf + f)),
            pl.BlockSpec((1, tf), lambda i, f: (0, f)),
            pl.BlockSpec((1, tf), lambda i, f: (0, nf + f)),
            pl.BlockSpec((tf, d), lambda i, f: (f, 0)),
            pl.BlockSpec((1, d), lambda i, f: (0, 0)),
        ],
        out_specs=pl.BlockSpec((tm, d), lambda i, f: (i, 0)),
        out_shape=jax.ShapeDtypeStruct((t, d), F32),
        scratch_shapes=[pltpu.VMEM((FFN_HALO + tm, d), BF16)],
        compiler_params=_params(("parallel", "arbitrary")),
        name=name,
    )(h, h, g, w_up, w_up, conv_w, conv_w, conv_b, conv_b, w_down, g_final)


def _rope(y, cos_t, sin_lo, sin_hi):
    outs = []
    for gidx in range(y.shape[1] // LANES):
        blk = y[:, gidx * LANES:(gidx + 1) * LANES]
        half = ROPE_DIM // 2
        outs.append(blk * cos_t + pltpu.roll(blk, half, 1) * sin_hi + pltpu.roll(blk, LANES - half, 1) * sin_lo)
    return jnp.concatenate(outs, axis=1)


def _kvq_kernel(x_ref, pos_ref, invf_ref, gkv_ref, gq_ref, w_ref, o_ref, xkv_ref, xq_ref, cos_ref, slo_ref, shi_ref,
                *, k_tiles, kv_tiles):
    j = pl.program_id(1)

    @pl.when(j == 0)
    def _():
        y = _rms_scale(x_ref[...])
        xkv_ref[...] = (y * gkv_ref[...]).astype(BF16)
        xq_ref[...] = (y * gq_ref[...]).astype(BF16)
        ang = pos_ref[...].astype(F32) * invf_ref[...]
        lane = lax.broadcasted_iota(jnp.int32, ang.shape, 1)
        half = ROPE_DIM // 2
        sin = jnp.sin(ang)
        cos_ref[...] = jnp.cos(ang)
        slo_ref[...] = jnp.where(lane < half, -sin, 0.0)
        shi_ref[...] = jnp.where((lane >= half) & (lane < ROPE_DIM), sin, 0.0)

    @pl.when(j < k_tiles)
    def _():
        y = jnp.dot(xkv_ref[...], w_ref[...], preferred_element_type=F32)
        o_ref[...] = _rope(y, cos_ref[...], slo_ref[...], shi_ref[...]).astype(BF16)

    @pl.when((j >= k_tiles) & (j < kv_tiles))
    def _():
        o_ref[...] = jnp.dot(xkv_ref[...], w_ref[...], preferred_element_type=F32).astype(BF16)

    @pl.when(j >= kv_tiles)
    def _():
        y = jnp.dot(xq_ref[...], w_ref[...], preferred_element_type=F32)
        y = _rope(y, cos_ref[...], slo_ref[...], shi_ref[...]) * (A_QK_DIM ** -0.5)
        o_ref[...] = y.astype(BF16)


def _kvq_proj(h, pos, inv_freq, g_kv, g_q, w_cat, k_cols, kv_cols):
    t, d = h.shape
    n = w_cat.shape[1]
    tm, tn = PROJ_TM, KVQ_TN
    kern = functools.partial(_kvq_kernel, k_tiles=k_cols // tn, kv_tiles=kv_cols // tn)
    return pl.pallas_call(
        kern,
        grid=(t // tm, n // tn),
        in_specs=[
            pl.BlockSpec((tm, d), lambda i, j: (i, 0)),
            pl.BlockSpec((tm, 1), lambda i, j: (i, 0)),
            pl.BlockSpec((1, LANES), lambda i, j: (0, 0)),
            pl.BlockSpec((1, d), lambda i, j: (0, 0)),
            pl.BlockSpec((1, d), lambda i, j: (0, 0)),
            pl.BlockSpec((d, tn), lambda i, j: (0, j)),
        ],
        out_specs=pl.BlockSpec((tm, tn), lambda i, j: (i, j)),
        out_shape=jax.ShapeDtypeStruct((t, n), BF16),
        scratch_shapes=[
            pltpu.VMEM((tm, d), BF16),
            pltpu.VMEM((tm, d), BF16),
            pltpu.VMEM((tm, LANES), F32),
            pltpu.VMEM((tm, LANES), F32),
            pltpu.VMEM((tm, LANES), F32),
        ],
        compiler_params=_params(("parallel", "arbitrary")),
        name="norm_kvq_rope",
    )(h, pos, inv_freq, g_kv, g_q, w_cat)


def _diff_attn_kernel(q_ref, k_ref, v_ref, lam_ref, g_ref, o_ref, acc_ref, m_ref, l_ref, *, lambda_init):
    qi = pl.program_id(2)
    tq = q_ref.shape[0]
    dk = A_QK_DIM

    q = q_ref[...]
    qs = (q[:, :dk], q[:, dk:])
    m_ref[...] = jnp.full(m_ref.shape, -jnp.inf, F32)
    l_ref[...] = jnp.zeros(l_ref.shape, F32)
    acc_ref[...] = jnp.zeros(acc_ref.shape, F32)

    def step(kb, masked):
        r0 = pl.multiple_of(kb * tq, tq)
        k = k_ref[pl.ds(r0, tq), :]
        v = v_ref[pl.ds(r0, tq), :]
        for c in range(2):
            s = lax.dot_general(qs[c], k[:, c * dk:(c + 1) * dk], (((1,), (1,)), ((), ())),
                                preferred_element_type=F32)
            if masked:
                row = lax.broadcasted_iota(jnp.int32, s.shape, 0)
                col = lax.broadcasted_iota(jnp.int32, s.shape, 1)
                s = jnp.where(row >= col, s, -jnp.inf)
            m_old = m_ref[c]
            m_new = jnp.maximum(m_old, jnp.max(s, axis=1, keepdims=True))
            alpha = jnp.exp(m_old - m_new)
            p = jnp.exp(s - m_new)
            l_ref[c] = alpha * l_ref[c] + jnp.sum(p, axis=1, keepdims=True)
            acc_ref[c] = alpha * acc_ref[c] + jnp.dot(p.astype(BF16), v, preferred_element_type=F32)
            m_ref[c] = m_new

    def body(kb, carry):
        step(kb, masked=False)
        return carry

    lax.fori_loop(0, qi, body, 0)
    step(qi, masked=True)

    lv = lam_ref[...]
    lam = (jnp.exp(jnp.sum(lv[0:1, :] * lv[1:2, :], axis=1, keepdims=True))
           - jnp.exp(jnp.sum(lv[2:3, :] * lv[3:4, :], axis=1, keepdims=True)) + lambda_init)
    o = acc_ref[0] / l_ref[0] - lam * (acc_ref[1] / l_ref[1])
    o_ref[...] = (_rms_scale(o) * g_ref[...] * (1.0 - lambda_init)).astype(o_ref.dtype)


def _diff_attention(kvq, lam_vecs, g_subln, batch, seq, lambda_init):
    tq = ATT_TQ
    nq = seq // tq
    hv = A_V_DIM
    v_off = A_HEADS
    q_off = 2 * A_HEADS
    return pl.pallas_call(
        functools.partial(_diff_attn_kernel, lambda_init=lambda_init),
        grid=(batch, A_HEADS, nq),
        in_specs=[
            pl.BlockSpec((tq, hv), lambda b, h, qi: (b * nq + qi, q_off + h)),
            pl.BlockSpec((seq, hv), lambda b, h, qi: (b, h)),
            pl.BlockSpec((seq, hv), lambda b, h, qi: (b, v_off + h)),
            pl.BlockSpec((4, A_QK_DIM), lambda b, h, qi: (0, 0)),
            pl.BlockSpec((1, hv), lambda b, h, qi: (0, 0)),
        ],
        out_specs=pl.BlockSpec((tq, hv), lambda b, h, qi: (b * nq + qi, h)),
        out_shape=jax.ShapeDtypeStruct((batch * seq, A_HEADS * hv), BF16),
        scratch_shapes=[
            pltpu.VMEM((2, tq, hv), F32),
            pltpu.VMEM((2, tq, 1), F32),
            pltpu.VMEM((2, tq, 1), F32),
        ],
        compiler_params=_params(("parallel", "parallel", "arbitrary")),
        name="diff_attention",
    )(kvq, kvq, kvq, lam_vecs, g_subln)


def kernel(x, positions, a_norm, m_w_in, m_b_igate, m_b_fgate, m_w_hnorm, m_w_out, kv_norm, w_kv, b_norm, w_q, lam_q1, lam_k1, lam_q2, lam_k2, subln, w_o, f_norm, w_up, conv_w, conv_b, w_down, final_norm):
    batch, seq, d = x.shape
    t = batch * seq
    depth = f_norm.shape[0]
    assert depth == 2 and a_norm.shape[0] == 1 and b_norm.shape[0] == 1
    assert seq % FFN_TM == 0 and seq % ATT_TQ == 0 and seq % M_CHUNK == 0 and t % PROJ_TM == 0

    h = x.reshape(t, d)

    n_in = m_w_in.shape[2]
    n_pad = -(-n_in // W_IN_TN) * W_IN_TN
    w_in = jnp.pad(m_w_in[0], ((0, 0), (0, n_pad - n_in))).astype(BF16)
    proj = _norm_matmul(h, a_norm[0][None, :], w_in, PROJ_TM, W_IN_TN)

    gate_col = 2 * M_HEADS * M_QK_DIM + 2 * M_HEADS * M_V_DIM
    nc = seq // M_CHUNK
    gates = proj[:, gate_col:gate_col + 2 * M_HEADS].reshape(batch, nc, M_CHUNK, 2, M_HEADS)
    gates_r = gates.transpose(0, 4, 3, 1, 2)
    gates_c = gates.transpose(0, 4, 1, 2, 3)
    bias = jnp.stack([m_b_igate[0], m_b_fgate[0]], axis=1)
    hg = _mlstm(proj, gates_r, gates_c, bias[:, :, None], bias[:, None, :], m_w_hnorm[0][:, None, :], batch, seq)
    h = _matmul_residual(hg, m_w_out[0].astype(BF16), h, PROJ_TM, RES_TN, "w_out_residual")

    ones = jnp.ones((1, d), F32)
    h = _conv_ffn(h, f_norm[0][None, :], w_up[0].astype(BF16), conv_w[0], conv_b[0][None, :],
                  w_down[0].astype(BF16), ones, seq, False, "conv_ffn_0")

    layer = 1
    lambda_init = 0.8 - 0.6 * math.exp(-0.3 * layer)
    half = ROPE_DIM // 2
    inv_freq = ROPE_THETA ** (-jnp.arange(half, dtype=F32) / half)
    inv_freq = jnp.concatenate([inv_freq, inv_freq, jnp.zeros((LANES - ROPE_DIM,), F32)])[None, :]
    w_cat = jnp.concatenate([w_kv, w_q[0]], axis=1).astype(BF16)
    k_cols = A_HEADS * 2 * A_QK_DIM
    kvq = _kvq_proj(h, positions.reshape(t, 1), inv_freq, kv_norm[None, :], b_norm[0][None, :], w_cat,
                    k_cols, w_kv.shape[1])
    lam_vecs = jnp.stack([lam_q1[0], lam_k1[0], lam_q2[0], lam_k2[0]], axis=0)
    att = _diff_attention(kvq, lam_vecs, subln[0][None, :], batch, seq, lambda_init)
    h = _matmul_residual(att, w_o[0].astype(BF16), h, PROJ_TM, RES_TN, "w_o_residual")

    h = _conv_ffn(h, f_norm[1][None, :], w_up[1].astype(BF16), conv_w[1], conv_b[1][None, :],
                  w_down[1].astype(BF16), final_norm[None, :], seq, True, "conv_ffn_1")
    return h.reshape(batch, seq, d)
```

```python
import functools
import math

import jax
import jax.numpy as jnp
from jax import lax
from jax.experimental import pallas as pl
from jax.experimental.pallas import tpu as pltpu

F32 = jnp.float32
BF16 = jnp.bfloat16

EPS = 1e-6
M_HEADS = 8
M_QK_DIM = 128
M_V_DIM = 256
GATE_SOFTCAP = 15.0
A_HEADS = 8
A_QK_DIM = 128
A_V_DIM = 256
ROPE_DIM = 32
ROPE_THETA = 500000.0
CONV_W = 3

LANES = 128
BF16_SUBLANES = 16
V7X_VMEM_LIMIT_BYTES = 56 * 1024 * 1024

PROJ_TM = 1024
W_IN_TN = 896
KQ_TN = 512
VT_TN = 512
RES_TN = 512
FFN_TM = 512
FFN_TF = 512
FFN_HALO = BF16_SUBLANES
ATT_T = 256
MLSTM_L = LANES

NT_DIMS = (((1,), (1,)), ((), ()))


def _rms_scale(x):
    return x * lax.rsqrt(jnp.mean(x * x, axis=-1, keepdims=True) + EPS)


def _params(semantics):
    return pltpu.CompilerParams(dimension_semantics=semantics, vmem_limit_bytes=V7X_VMEM_LIMIT_BYTES)


def _norm_matmul_kernel(x_ref, g_ref, w_ref, o_ref, xn_ref):
    @pl.when(pl.program_id(1) == 0)
    def _():
        xn_ref[...] = (_rms_scale(x_ref[...]) * g_ref[...]).astype(BF16)

    o_ref[...] = jnp.dot(xn_ref[...], w_ref[...], preferred_element_type=F32)


def _norm_matmul(x, g, w, tm, tn):
    t, d = x.shape
    n = w.shape[1]
    return pl.pallas_call(
        _norm_matmul_kernel,
        grid=(t // tm, n // tn),
        in_specs=[
            pl.BlockSpec((tm, d), lambda i, j: (i, 0)),
            pl.BlockSpec((1, d), lambda i, j: (0, 0)),
            pl.BlockSpec((d, tn), lambda i, j: (0, j)),
        ],
        out_specs=pl.BlockSpec((tm, tn), lambda i, j: (i, j)),
        out_shape=jax.ShapeDtypeStruct((t, n), F32),
        scratch_shapes=[pltpu.VMEM((tm, d), BF16)],
        compiler_params=_params(("parallel", "arbitrary")),
        name="norm_w_in",
    )(x, g, w)


def _softcap(t):
    return GATE_SOFTCAP * jnp.tanh(t / GATE_SOFTCAP)


def _log_sigmoid(x):
    return jnp.minimum(x, 0.0) - jnp.log1p(jnp.exp(-jnp.abs(x)))


def _lane_scan(x, op, fill):
    lane = lax.broadcasted_iota(jnp.int32, x.shape, 1)
    d = 1
    while d < x.shape[1]:
        x = op(x, jnp.where(lane >= d, pltpu.roll(x, d, 1), fill))
        d *= 2
    return x


def _mlstm_kernel(q_ref, k_ref, v_ref, o_ref, gr_ref, br_ref, wn_ref, out_ref, *, seq):
    L = MLSTM_L
    nc = seq // L
    dk, dv = M_QK_DIM, M_V_DIM
    scale = dk ** -0.5
    bias = br_ref[...]
    wn = wn_ref[...]

    gr = gr_ref[...]
    i_rows = _softcap(gr[0] + bias[0:1, :])
    f_rows = _log_sigmoid(_softcap(gr[1] + bias[1:2, :]))
    b_rows = _lane_scan(f_rows, jnp.add, 0.0)
    a_rows = i_rows - b_rows
    pm_rows = _lane_scan(a_rows, jnp.maximum, -jnp.inf)
    b_last = b_rows[:, L - 1:L]
    a_max = pm_rows[:, L - 1:L]

    m = jnp.zeros((1, 1), F32)
    ms = [m]
    for c in range(nc):
        m = b_last[c:c + 1, :] + jnp.maximum(m, a_max[c:c + 1, :])
        ms.append(m)
    m_in = jnp.concatenate(ms[:nc], axis=0)
    m_out = jnp.concatenate(ms[1:], axis=0)
    big_m_rows = jnp.maximum(m_in, pm_rows)
    decay = jnp.exp(b_last + m_in - m_out)

    stacked = jnp.concatenate([
        big_m_rows,
        jnp.exp(m_in - big_m_rows),
        jnp.exp(-(b_rows + big_m_rows)),
        jnp.exp(a_rows + b_last - m_out),
        jnp.zeros((L - 4 * nc, L), F32)], axis=0)
    cols = stacked.T

    row_t = lax.broadcasted_iota(jnp.int32, (L, L), 0)
    col_s = lax.broadcasted_iota(jnp.int32, (L, L), 1)
    causal = row_t >= col_s
    ones_blk = jnp.ones((L, LANES), BF16)

    state = jnp.zeros((dk, dv + LANES), F32)
    for c in range(nc):
        rows = slice(c * L, (c + 1) * L)
        qb = (q_ref[rows, :] * scale).astype(BF16)
        k = k_ref[rows, :]
        kb = k.astype(BF16)
        v_aug = jnp.concatenate([v_ref[rows, :].astype(BF16), ones_blk], axis=1)
        big_m = cols[:, c:c + 1]
        w_inter = cols[:, nc + c:nc + c + 1]
        e_neg_m = cols[:, 2 * nc + c:2 * nc + c + 1]
        w_key = cols[:, 3 * nc + c:3 * nc + c + 1]

        e = jnp.exp(jnp.where(causal, a_rows[c:c + 1, :] - big_m, -jnp.inf))
        s = lax.dot_general(qb, kb, NT_DIMS, preferred_element_type=F32) * e
        nd = (w_inter * jnp.dot(qb, state.astype(BF16), preferred_element_type=F32)
              + jnp.dot(s.astype(BF16), v_aug, preferred_element_type=F32))
        den = nd[:, dv:]
        r = 1.0 / jnp.maximum(jnp.abs(den), e_neg_m)
        h = nd[:, :dv] * jnp.concatenate([r] * (dv // LANES), axis=1)

        hn = _rms_scale(h) * wn
        out_ref[rows, :] = (hn * jax.nn.sigmoid(o_ref[rows, :])).astype(out_ref.dtype)

        kw_t = (k * w_key).T.astype(BF16)
        state = decay[c:c + 1, :] * state + jnp.dot(kw_t, v_aug, preferred_element_type=F32)


def _mlstm(proj, gates_r, bias_r, w_hnorm, batch, seq):
    nc = seq // MLSTM_L
    qk_blocks = M_HEADS * M_QK_DIM // M_QK_DIM
    v_off = 2 * M_HEADS * M_QK_DIM // M_V_DIM
    o_off = v_off + M_HEADS
    return pl.pallas_call(
        functools.partial(_mlstm_kernel, seq=seq),
        grid=(batch, M_HEADS),
        in_specs=[
            pl.BlockSpec((seq, M_QK_DIM), lambda b, h: (b, h)),
            pl.BlockSpec((seq, M_QK_DIM), lambda b, h: (b, qk_blocks + h)),
            pl.BlockSpec((seq, M_V_DIM), lambda b, h: (b, v_off + h)),
            pl.BlockSpec((seq, M_V_DIM), lambda b, h: (b, o_off + h)),
            pl.BlockSpec((None, None, 2, nc, MLSTM_L), lambda b, h: (b, h, 0, 0, 0)),
            pl.BlockSpec((None, 2, 1), lambda b, h: (h, 0, 0)),
            pl.BlockSpec((None, 1, M_V_DIM), lambda b, h: (h, 0, 0)),
        ],
        out_specs=pl.BlockSpec((seq, M_V_DIM), lambda b, h: (b, h)),
        out_shape=jax.ShapeDtypeStruct((batch * seq, M_HEADS * M_V_DIM), BF16),
        compiler_params=_params(("parallel", "parallel")),
        name="mlstm_chunkwise",
    )(proj, proj, proj, proj, gates_r, bias_r, w_hnorm)


def _matmul_residual_kernel(a_ref, w_ref, r_ref, o_ref):
    o_ref[...] = r_ref[...] + jnp.dot(a_ref[...], w_ref[...], preferred_element_type=F32)


def _matmul_residual(a, w, res, tm, tn, name):
    t, kdim = a.shape
    n = w.shape[1]
    return pl.pallas_call(
        _matmul_residual_kernel,
        grid=(t // tm, n // tn),
        in_specs=[
            pl.BlockSpec((tm, kdim), lambda i, j: (i, 0)),
            pl.BlockSpec((kdim, tn), lambda i, j: (0, j)),
            pl.BlockSpec((tm, tn), lambda i, j: (i, j)),
        ],
        out_specs=pl.BlockSpec((tm, tn), lambda i, j: (i, j)),
        out_shape=jax.ShapeDtypeStruct((t, n), F32),
        compiler_params=_params(("parallel", "parallel")),
        name=name,
    )(a, w, res)


def _conv_ffn_kernel(h_ref, hp_ref, g_ref, wg_ref, wv_ref, cwg_ref, cwv_ref, cbg_ref, cbv_ref, wd_ref,
                     gf_ref, o_ref, xn_ref, *, tiles_per_seq, final_norm):
    i = pl.program_id(0)
    f = pl.program_id(1)

    @pl.when(f == 0)
    def _():
        g = g_ref[...]
        x = h_ref[...]
        xn_ref[FFN_HALO:, :] = (_rms_scale(x) * g).astype(BF16)
        keep = (i % tiles_per_seq != 0).astype(F32)
        xn_ref[:FFN_HALO, :] = (_rms_scale(hp_ref[...]) * g * keep).astype(BF16)
        o_ref[...] = x

    xn = xn_ref[...]

    def branch(w_ref, cw_ref, cb_ref):
        u = jnp.dot(xn, w_ref[...], preferred_element_type=F32)
        cw = cw_ref[...]
        c = cb_ref[...] + pltpu.roll(u, 2, 0) * cw[0:1, :] + pltpu.roll(u, 1, 0) * cw[1:2, :] + u * cw[2:3, :]
        return c[FFN_HALO:, :]

    gate = branch(wg_ref, cwg_ref, cbg_ref)
    val = branch(wv_ref, cwv_ref, cbv_ref)
    act = (gate * jax.nn.sigmoid(gate) * val).astype(BF16)
    o_ref[...] += jnp.dot(act, wd_ref[...], preferred_element_type=F32)

    if final_norm:
        @pl.when(f == pl.num_programs(1) - 1)
        def _():
            o_ref[...] = _rms_scale(o_ref[...]) * gf_ref[...]


def _conv_ffn(h, g, w_up, conv_w, conv_b, w_down, g_final, seq, final_norm, name):
    t, d = h.shape
    d_ff = w_down.shape[0]
    tm, tf = FFN_TM, FFN_TF
    nf = d_ff // tf
    halo_blocks = tm // FFN_HALO
    kern = functools.partial(_conv_ffn_kernel, tiles_per_seq=seq // tm, final_norm=final_norm)
    return pl.pallas_call(
        kern,
        grid=(t // tm, nf),
        in_specs=[
            pl.BlockSpec((tm, d), lambda i, f: (i, 0)),
            pl.BlockSpec((FFN_HALO, d), lambda i, f: (jnp.maximum(i * halo_blocks - 1, 0), 0)),
            pl.BlockSpec((1, d), lambda i, f: (0, 0)),
            pl.BlockSpec((d, tf), lambda i, f: (0, f)),
            pl.BlockSpec((d, tf), lambda i, f: (0, nf + f)),
            pl.BlockSpec((CONV_W, tf), lambda i, f: (0, f)),
            pl.BlockSpec((CONV_W, tf), lambda i, f: (0, nf + f)),
            pl.BlockSpec((1, tf), lambda i, f: (0, f)),
            pl.BlockSpec((1, tf), lambda i, f: (0, nf + f)),
            pl.BlockSpec((tf, d), lambda i, f: (f, 0)),
            pl.BlockSpec((1, d), lambda i, f: (0, 0)),
        ],
        out_specs=pl.BlockSpec((tm, d), lambda i, f: (i, 0)),
        out_shape=jax.ShapeDtypeStruct((t, d), F32),
        scratch_shapes=[pltpu.VMEM((FFN_HALO + tm, d), BF16)],
        compiler_params=_params(("parallel", "arbitrary")),
        name=name,
    )(h, h, g, w_up, w_up, conv_w, conv_w, conv_b, conv_b, w_down, g_final)


def _rope(y, cos_t, sin_lo, sin_hi):
    outs = []
    for gidx in range(y.shape[1] // LANES):
        blk = y[:, gidx * LANES:(gidx + 1) * LANES]
        half = ROPE_DIM // 2
        outs.append(blk * cos_t + pltpu.roll(blk, half, 1) * sin_hi + pltpu.roll(blk, LANES - half, 1) * sin_lo)
    return jnp.concatenate(outs, axis=1)


def _kq_kernel(x_ref, pos_ref, invf_ref, gkv_ref, gq_ref, w_ref, o_ref, xkv_ref, xq_ref, cos_ref, slo_ref, shi_ref,
               *, k_tiles):
    j = pl.program_id(1)

    @pl.when(j == 0)
    def _():
        y = _rms_scale(x_ref[...])
        xkv_ref[...] = (y * gkv_ref[...]).astype(BF16)
        xq_ref[...] = (y * gq_ref[...]).astype(BF16)
        ang = pos_ref[...].astype(F32) * invf_ref[...]
        lane = lax.broadcasted_iota(jnp.int32, ang.shape, 1)
        half = ROPE_DIM // 2
        sin = jnp.sin(ang)
        cos_ref[...] = jnp.cos(ang)
        slo_ref[...] = jnp.where(lane < half, -sin, 0.0)
        shi_ref[...] = jnp.where((lane >= half) & (lane < ROPE_DIM), sin, 0.0)

    @pl.when(j < k_tiles)
    def _():
        y = jnp.dot(xkv_ref[...], w_ref[...], preferred_element_type=F32)
        o_ref[...] = _rope(y, cos_ref[...], slo_ref[...], shi_ref[...]).astype(BF16)

    @pl.when(j >= k_tiles)
    def _():
        y = jnp.dot(xq_ref[...], w_ref[...], preferred_element_type=F32)
        y = _rope(y, cos_ref[...], slo_ref[...], shi_ref[...]) * (A_QK_DIM ** -0.5)
        o_ref[...] = y.astype(BF16)


def _kq_proj(h, pos, inv_freq, g_kv, g_q, w_cat, k_cols):
    t, d = h.shape
    n = w_cat.shape[1]
    tm, tn = PROJ_TM, KQ_TN
    kern = functools.partial(_kq_kernel, k_tiles=k_cols // tn)
    return pl.pallas_call(
        kern,
        grid=(t // tm, n // tn),
        in_specs=[
            pl.BlockSpec((tm, d), lambda i, j: (i, 0)),
            pl.BlockSpec((tm, 1), lambda i, j: (i, 0)),
            pl.BlockSpec((1, LANES), lambda i, j: (0, 0)),
            pl.BlockSpec((1, d), lambda i, j: (0, 0)),
            pl.BlockSpec((1, d), lambda i, j: (0, 0)),
            pl.BlockSpec((d, tn), lambda i, j: (0, j)),
        ],
        out_specs=pl.BlockSpec((tm, tn), lambda i, j: (i, j)),
        out_shape=jax.ShapeDtypeStruct((t, n), BF16),
        scratch_shapes=[
            pltpu.VMEM((tm, d), BF16),
            pltpu.VMEM((tm, d), BF16),
            pltpu.VMEM((tm, LANES), F32),
            pltpu.VMEM((tm, LANES), F32),
            pltpu.VMEM((tm, LANES), F32),
        ],
        compiler_params=_params(("parallel", "arbitrary")),
        name="norm_kq_rope",
    )(h, pos, inv_freq, g_kv, g_q, w_cat)


def _vt_kernel(x_ref, g_ref, wt_ref, o_ref, xn_ref):
    @pl.when(pl.program_id(1) == 0)
    def _():
        xn_ref[...] = (_rms_scale(x_ref[...]) * g_ref[...]).astype(BF16)

    o_ref[...] = lax.dot_general(wt_ref[...], xn_ref[...], NT_DIMS, preferred_element_type=F32).astype(BF16)


def _vt_proj(h, g, w_t):
    t, d = h.shape
    n = w_t.shape[0]
    tm, tn = PROJ_TM, VT_TN
    return pl.pallas_call(
        _vt_kernel,
        grid=(t // tm, n // tn),
        in_specs=[
            pl.BlockSpec((tm, d), lambda i, j: (i, 0)),
            pl.BlockSpec((1, d), lambda i, j: (0, 0)),
            pl.BlockSpec((tn, d), lambda i, j: (j, 0)),
        ],
        out_specs=pl.BlockSpec((tn, tm), lambda i, j: (j, i)),
        out_shape=jax.ShapeDtypeStruct((n, t), BF16),
        scratch_shapes=[pltpu.VMEM((tm, d), BF16)],
        compiler_params=_params(("parallel", "arbitrary")),
        name="norm_v_transposed",
    )(h, g, w_t)


def _diff_attn_kernel(k_ref, q_ref, vt_ref, lam_ref, g_ref, o_ref, *, lambda_init, seq):
    tb = ATT_T
    dk = A_QK_DIM
    nblk = seq // tb

    lv = lam_ref[...]
    lam = (jnp.exp(jnp.sum(lv[0:1, :] * lv[1:2, :], axis=1, keepdims=True))
           - jnp.exp(jnp.sum(lv[2:3, :] * lv[3:4, :], axis=1, keepdims=True)) + lambda_init)
    g_col = g_ref[...] * (1.0 - lambda_init)
    key_idx = lax.broadcasted_iota(jnp.int32, (tb, tb), 0)
    qry_idx = lax.broadcasted_iota(jnp.int32, (tb, tb), 1)
    diag_mask = key_idx <= qry_idx

    for qi in range(nblk):
        q = q_ref[qi * tb:(qi + 1) * tb, :]
        comps = []
        for c in range(2):
            qc = q[:, c * dk:(c + 1) * dk]
            m = l = acc = None
            for kb in range(qi + 1):
                kc = k_ref[kb * tb:(kb + 1) * tb, c * dk:(c + 1) * dk]
                vt = vt_ref[:, kb * tb:(kb + 1) * tb]
                s = lax.dot_general(kc, qc, NT_DIMS, preferred_element_type=F32)
                if kb == qi:
                    s = jnp.where(diag_mask, s, -jnp.inf)
                s_max = jnp.max(s, axis=0, keepdims=True)
                if kb == 0:
                    m = s_max
                    p = jnp.exp(s - m)
                    l = jnp.sum(p, axis=0, keepdims=True)
                    acc = jnp.dot(vt, p.astype(BF16), preferred_element_type=F32)
                else:
                    m_new = jnp.maximum(m, s_max)
                    alpha = jnp.exp(m - m_new)
                    p = jnp.exp(s - m_new)
                    l = alpha * l + jnp.sum(p, axis=0, keepdims=True)
                    acc = alpha * acc + jnp.dot(vt, p.astype(BF16), preferred_element_type=F32)
                    m = m_new
            comps.append(acc * (1.0 / l))
        o_t = comps[0] - lam * comps[1]
        o_t = o_t * lax.rsqrt(jnp.mean(o_t * o_t, axis=0, keepdims=True) + EPS) * g_col
        o_ref[qi * tb:(qi + 1) * tb, :] = o_t.T.astype(o_ref.dtype)


def _diff_attention(kq, v_t, lam_vecs, g_subln_col, batch, seq, lambda_init):
    hv = A_V_DIM
    return pl.pallas_call(
        functools.partial(_diff_attn_kernel, lambda_init=lambda_init, seq=seq),
        grid=(batch, A_HEADS),
        in_specs=[
            pl.BlockSpec((seq, hv), lambda b, h: (b, h)),
            pl.BlockSpec((seq, hv), lambda b, h: (b, A_HEADS + h)),
            pl.BlockSpec((hv, seq), lambda b, h: (h, b)),
            pl.BlockSpec((4, A_QK_DIM), lambda b, h: (0, 0)),
            pl.BlockSpec((hv, 1), lambda b, h: (0, 0)),
        ],
        out_specs=pl.BlockSpec((seq, hv), lambda b, h: (b, h)),
        out_shape=jax.ShapeDtypeStruct((batch * seq, A_HEADS * hv), BF16),
        compiler_params=_params(("parallel", "parallel")),
        name="diff_attention",
    )(kq, kq, v_t, lam_vecs, g_subln_col)


def kernel(x, positions, a_norm, m_w_in, m_b_igate, m_b_fgate, m_w_hnorm, m_w_out, kv_norm, w_kv, b_norm, w_q, lam_q1, lam_k1, lam_q2, lam_k2, subln, w_o, f_norm, w_up, conv_w, conv_b, w_down, final_norm):
    batch, seq, d = x.shape
    t = batch * seq
    depth = f_norm.shape[0]
    assert depth == 2 and a_norm.shape[0] == 1 and b_norm.shape[0] == 1
    assert seq % FFN_TM == 0 and seq % ATT_T == 0 and t % PROJ_TM == 0
    assert seq % MLSTM_L == 0 and 4 * (seq // MLSTM_L) <= MLSTM_L

    h = x.reshape(t, d)

    n_in = m_w_in.shape[2]
    n_pad = -(-n_in // W_IN_TN) * W_IN_TN
    w_in = jnp.pad(m_w_in[0], ((0, 0), (0, n_pad - n_in))).astype(BF16)
    proj = _norm_matmul(h, a_norm[0][None, :], w_in, PROJ_TM, W_IN_TN)

    gate_col = 2 * M_HEADS * M_QK_DIM + 2 * M_HEADS * M_V_DIM
    nc = seq // MLSTM_L
    gates = proj[:, gate_col:gate_col + 2 * M_HEADS].reshape(batch, nc, MLSTM_L, 2, M_HEADS)
    gates_r = gates.transpose(0, 4, 3, 1, 2)
    bias = jnp.stack([m_b_igate[0], m_b_fgate[0]], axis=1)
    hg = _mlstm(proj, gates_r, bias[:, :, None], m_w_hnorm[0][:, None, :], batch, seq)
    h = _matmul_residual(hg, m_w_out[0].astype(BF16), h, PROJ_TM, RES_TN, "w_out_residual")

    ones = jnp.ones((1, d), F32)
    h = _conv_ffn(h, f_norm[0][None, :], w_up[0].astype(BF16), conv_w[0], conv_b[0][None, :],
                  w_down[0].astype(BF16), ones, seq, False, "conv_ffn_0")

    layer = 1
    lambda_init = 0.8 - 0.6 * math.exp(-0.3 * layer)
    half = ROPE_DIM // 2
    inv_freq = ROPE_THETA ** (-jnp.arange(half, dtype=F32) / half)
    inv_freq = jnp.concatenate([inv_freq, inv_freq, jnp.zeros((LANES - ROPE_DIM,), F32)])[None, :]
    k_cols = A_HEADS * 2 * A_QK_DIM
    w_kq = jnp.concatenate([w_kv[:, :k_cols], w_q[0]], axis=1).astype(BF16)
    w_vt = w_kv[:, k_cols:].T.astype(BF16)
    kq = _kq_proj(h, positions.reshape(t, 1), inv_freq, kv_norm[None, :], b_norm[0][None, :], w_kq, k_cols)
    v_t = _vt_proj(h, kv_norm[None, :], w_vt)
    lam_vecs = jnp.stack([lam_q1[0], lam_k1[0], lam_q2[0], lam_k2[0]], axis=0)
    att = _diff_attention(kq, v_t, lam_vecs, subln[0][:, None], batch, seq, lambda_init)
    h = _matmul_residual(att, w_o[0].astype(BF16), h, PROJ_TM, RES_TN, "w_o_residual")

    h = _conv_ffn(h, f_norm[1][None, :], w_up[1].astype(BF16), conv_w[1], conv_b[1][None, :],
                  w_down[1].astype(BF16), final_norm[None, :], seq, True, "conv_ffn_1")
    return h.reshape(batch, seq, d)
```

```python
import functools
import math

import jax
import jax.numpy as jnp
from jax import lax
from jax.experimental import pallas as pl
from jax.experimental.pallas import tpu as pltpu

F32 = jnp.float32
BF16 = jnp.bfloat16

EPS = 1e-6
M_HEADS = 8
M_QK_DIM = 128
M_V_DIM = 256
GATE_SOFTCAP = 15.0
A_HEADS = 8
A_QK_DIM = 128
A_V_DIM = 256
ROPE_DIM = 32
ROPE_THETA = 500000.0
CONV_W = 3

LANES = 128
BF16_SUBLANES = 16
V7X_VMEM_LIMIT_BYTES = 56 * 1024 * 1024

PROJ_TM = 1024
W_IN_TN = 1024
KQ_TN = 1024
VT_TN = 512
RES_TN = 512
FFN_TM = 512
FFN_TF = 512
FFN_TAIL = 8
ATT_T = 256
MLSTM_L = LANES

NT_DIMS = (((1,), (1,)), ((), ()))


def _rms_scale(x):
    return x * lax.rsqrt(jnp.mean(x * x, axis=-1, keepdims=True) + EPS)


def _params(semantics):
    return pltpu.CompilerParams(dimension_semantics=semantics, vmem_limit_bytes=V7X_VMEM_LIMIT_BYTES)


def _w_in_kernel(x_ref, g_ref, w_ref, wg_ref, qkv_ref, o_ref, gates_ref, xn_ref, *, qkv_tiles):
    j = pl.program_id(1)

    @pl.when(j == 0)
    def _():
        xn = (_rms_scale(x_ref[...]) * g_ref[...]).astype(BF16)
        xn_ref[...] = xn
        gates_ref[...] = jnp.dot(xn, wg_ref[...], preferred_element_type=F32)
        y = jnp.dot(xn, w_ref[...], preferred_element_type=F32)
        qkv_ref[...] = (y * (M_QK_DIM ** -0.5)).astype(BF16)

    @pl.when((j > 0) & (j < qkv_tiles))
    def _():
        qkv_ref[...] = jnp.dot(xn_ref[...], w_ref[...], preferred_element_type=F32).astype(BF16)

    @pl.when(j >= qkv_tiles)
    def _():
        o_ref[...] = jnp.dot(xn_ref[...], w_ref[...], preferred_element_type=F32)


def _w_in_proj(x, g, w, w_gates, qkv_cols, o_cols):
    t, d = x.shape
    tm, tn = PROJ_TM, W_IN_TN
    assert M_HEADS * M_QK_DIM == tn and qkv_cols % tn == 0 and o_cols % tn == 0
    qkv_tiles = qkv_cols // tn
    return pl.pallas_call(
        functools.partial(_w_in_kernel, qkv_tiles=qkv_tiles),
        grid=(t // tm, (qkv_cols + o_cols) // tn),
        in_specs=[
            pl.BlockSpec((tm, d), lambda i, j: (i, 0)),
            pl.BlockSpec((1, d), lambda i, j: (0, 0)),
            pl.BlockSpec((d, tn), lambda i, j: (0, j)),
            pl.BlockSpec((d, LANES), lambda i, j: (0, 0)),
        ],
        out_specs=[
            pl.BlockSpec((tm, tn), lambda i, j: (i, jnp.minimum(j, qkv_tiles - 1))),
            pl.BlockSpec((tm, tn), lambda i, j: (i, jnp.maximum(j - qkv_tiles, 0))),
            pl.BlockSpec((tm, LANES), lambda i, j: (i, 0)),
        ],
        out_shape=[
            jax.ShapeDtypeStruct((t, qkv_cols), BF16),
            jax.ShapeDtypeStruct((t, o_cols), F32),
            jax.ShapeDtypeStruct((t, LANES), F32),
        ],
        scratch_shapes=[pltpu.VMEM((tm, d), BF16)],
        compiler_params=_params(("parallel", "arbitrary")),
        name="norm_w_in",
    )(x, g, w, w_gates)


def _softcap(t):
    return GATE_SOFTCAP * jnp.tanh(t / GATE_SOFTCAP)


def _log_sigmoid(x):
    return jnp.minimum(x, 0.0) - jnp.log1p(jnp.exp(-jnp.abs(x)))


def _lane_scan(x, op, fill):
    lane = lax.broadcasted_iota(jnp.int32, x.shape, 1)
    d = 1
    while d < x.shape[1]:
        x = op(x, jnp.where(lane >= d, pltpu.roll(x, d, 1), fill))
        d *= 2
    return x


def _mlstm_kernel(q_ref, k_ref, v_ref, o_ref, gr_ref, br_ref, wn_ref, out_ref, *, seq):
    L = MLSTM_L
    nc = seq // L
    dk, dv = M_QK_DIM, M_V_DIM
    bias = br_ref[...]
    wn = wn_ref[...]

    gr = gr_ref[...]
    i_rows = _softcap(gr[0] + bias[0:1, :])
    f_rows = _log_sigmoid(_softcap(gr[1] + bias[1:2, :]))
    b_rows = _lane_scan(f_rows, jnp.add, 0.0)
    a_rows = i_rows - b_rows
    pm_rows = _lane_scan(a_rows, jnp.maximum, -jnp.inf)
    b_last = b_rows[:, L - 1:L]
    a_max = pm_rows[:, L - 1:L]

    m = jnp.zeros((1, 1), F32)
    ms = [m]
    for c in range(nc):
        m = b_last[c:c + 1, :] + jnp.maximum(m, a_max[c:c + 1, :])
        ms.append(m)
    m_in = jnp.concatenate(ms[:nc], axis=0)
    m_out = jnp.concatenate(ms[1:], axis=0)
    big_m_rows = jnp.maximum(m_in, pm_rows)
    decay = jnp.exp(b_last + m_in - m_out)

    stacked = jnp.concatenate([
        big_m_rows,
        jnp.exp(m_in - big_m_rows),
        jnp.exp(-(b_rows + big_m_rows)),
        jnp.exp(a_rows + b_last - m_out),
        jnp.zeros((L - 4 * nc, L), F32)], axis=0)
    cols = stacked.T

    row_t = lax.broadcasted_iota(jnp.int32, (L, L), 0)
    col_s = lax.broadcasted_iota(jnp.int32, (L, L), 1)
    causal = row_t >= col_s
    ones_blk = jnp.ones((L, LANES), BF16)

    state = jnp.zeros((dk, dv + LANES), F32)
    for c in range(nc):
        rows = slice(c * L, (c + 1) * L)
        qb = q_ref[rows, :]
        kb = k_ref[rows, :]
        v_aug = jnp.concatenate([v_ref[rows, :], ones_blk], axis=1)
        big_m = cols[:, c:c + 1]
        w_inter = cols[:, nc + c:nc + c + 1]
        e_neg_m = cols[:, 2 * nc + c:2 * nc + c + 1]
        w_key = cols[:, 3 * nc + c:3 * nc + c + 1]

        e = jnp.exp(jnp.where(causal, a_rows[c:c + 1, :] - big_m, -jnp.inf))
        s = lax.dot_general(qb, kb, NT_DIMS, preferred_element_type=F32) * e
        nd = (w_inter * jnp.dot(qb, state.astype(BF16), preferred_element_type=F32)
              + jnp.dot(s.astype(BF16), v_aug, preferred_element_type=F32))
        den = nd[:, dv:]
        r = 1.0 / jnp.maximum(jnp.abs(den), e_neg_m)
        h = nd[:, :dv] * jnp.concatenate([r] * (dv // LANES), axis=1)

        hn = _rms_scale(h) * wn
        out_ref[rows, :] = (hn * jax.nn.sigmoid(o_ref[rows, :])).astype(out_ref.dtype)

        kw_t = (kb.astype(F32) * w_key).T.astype(BF16)
        state = decay[c:c + 1, :] * state + jnp.dot(kw_t, v_aug, preferred_element_type=F32)


def _mlstm(qkv, o_gate, gates_r, bias_r, w_hnorm, batch, seq):
    nc = seq // MLSTM_L
    qk_blocks = M_HEADS * M_QK_DIM // M_QK_DIM
    v_off = 2 * M_HEADS * M_QK_DIM // M_V_DIM
    return pl.pallas_call(
        functools.partial(_mlstm_kernel, seq=seq),
        grid=(batch, M_HEADS),
        in_specs=[
            pl.BlockSpec((seq, M_QK_DIM), lambda b, h: (b, h)),
            pl.BlockSpec((seq, M_QK_DIM), lambda b, h: (b, qk_blocks + h)),
            pl.BlockSpec((seq, M_V_DIM), lambda b, h: (b, v_off + h)),
            pl.BlockSpec((seq, M_V_DIM), lambda b, h: (b, h)),
            pl.BlockSpec((None, None, 2, nc, MLSTM_L), lambda b, h: (b, h, 0, 0, 0)),
            pl.BlockSpec((None, 2, 1), lambda b, h: (h, 0, 0)),
            pl.BlockSpec((None, 1, M_V_DIM), lambda b, h: (h, 0, 0)),
        ],
        out_specs=pl.BlockSpec((seq, M_V_DIM), lambda b, h: (b, h)),
        out_shape=jax.ShapeDtypeStruct((batch * seq, M_HEADS * M_V_DIM), BF16),
        compiler_params=_params(("parallel", "parallel")),
        name="mlstm_chunkwise",
    )(qkv, qkv, qkv, o_gate, gates_r, bias_r, w_hnorm)


def _matmul_residual_kernel(a_ref, w_ref, r_ref, o_ref):
    o_ref[...] = r_ref[...] + jnp.dot(a_ref[...], w_ref[...], preferred_element_type=F32)


def _matmul_residual(a, w, res, tm, tn, name):
    t, kdim = a.shape
    n = w.shape[1]
    return pl.pallas_call(
        _matmul_residual_kernel,
        grid=(t // tm, n // tn),
        in_specs=[
            pl.BlockSpec((tm, kdim), lambda i, j: (i, 0)),
            pl.BlockSpec((kdim, tn), lambda i, j: (0, j)),
            pl.BlockSpec((tm, tn), lambda i, j: (i, j)),
        ],
        out_specs=pl.BlockSpec((tm, tn), lambda i, j: (i, j)),
        out_shape=jax.ShapeDtypeStruct((t, n), F32),
        compiler_params=_params(("parallel", "parallel")),
        name=name,
    )(a, w, res)


def _conv_ffn_kernel(h_ref, g_ref, wg_ref, wv_ref, cwg_ref, cwv_ref, cbg_ref, cbv_ref, wd_ref,
                     gf_ref, o_ref, xn_ref, tail_ref, act_ref, *, tiles_per_seq, final_norm):
    i = pl.program_id(0)
    f = pl.program_id(1)
    nf = pl.num_programs(1) - 1
    tm = h_ref.shape[0]

    @pl.when(f == 0)
    def _():
        x = h_ref[...]
        xn_ref[...] = (_rms_scale(x) * g_ref[...]).astype(BF16)
        o_ref[...] = x

    @pl.when((i % tiles_per_seq == 0) & (f < nf))
    def _():
        tail_ref[f] = jnp.zeros(tail_ref.shape[1:], F32)

    def conv(w_ref, cw_ref, cb_ref, slot):
        u = jnp.dot(xn_ref[...], w_ref[...], preferred_element_type=F32)
        ue = jnp.concatenate([tail_ref[f, slot], u], axis=0)
        tail_ref[f, slot] = u[tm - FFN_TAIL:, :]
        cw = cw_ref[...]
        c = cb_ref[...] + pltpu.roll(ue, 2, 0) * cw[0:1, :] + pltpu.roll(ue, 1, 0) * cw[1:2, :] + ue * cw[2:3, :]
        return c[FFN_TAIL:, :]

    def up(dst):
        gate = conv(wg_ref, cwg_ref, cbg_ref, 0)
        val = conv(wv_ref, cwv_ref, cbv_ref, 1)
        act_ref[dst] = (gate * jax.nn.sigmoid(gate) * val).astype(BF16)

    def down(src):
        o_ref[...] += jnp.dot(act_ref[src], wd_ref[...], preferred_element_type=F32)

    @pl.when(f == 0)
    def _():
        up(0)

    for parity in range(2):
        @pl.when((f > 0) & (f < nf) & (f % 2 == parity))
        def _():
            up(parity)
            down(1 - parity)

        @pl.when((f == nf) & (f % 2 == parity))
        def _():
            down(1 - parity)
            if final_norm:
                o_ref[...] = _rms_scale(o_ref[...]) * gf_ref[...]


def _conv_ffn(h, layer, g, w_up, conv_w, conv_b, w_down, g_final, seq, final_norm, name):
    t, d = h.shape
    d_ff = w_down.shape[1]
    tm, tf = FFN_TM, FFN_TF
    nf = d_ff // tf
    kern = functools.partial(_conv_ffn_kernel, tiles_per_seq=seq // tm, final_norm=final_norm)

    def up_tile(f):
        return jnp.minimum(f, nf - 1)

    return pl.pallas_call(
        kern,
        grid=(t // tm, nf + 1),
        in_specs=[
            pl.BlockSpec((tm, d), lambda i, f: (i, 0)),
            pl.BlockSpec((None, 1, d), lambda i, f: (layer, 0, 0)),
            pl.BlockSpec((None, d, tf), lambda i, f: (layer, 0, up_tile(f))),
            pl.BlockSpec((None, d, tf), lambda i, f: (layer, 0, nf + up_tile(f))),
            pl.BlockSpec((None, CONV_W, tf), lambda i, f: (layer, 0, up_tile(f))),
            pl.BlockSpec((None, CONV_W, tf), lambda i, f: (layer, 0, nf + up_tile(f))),
            pl.BlockSpec((None, 1, tf), lambda i, f: (layer, 0, up_tile(f))),
            pl.BlockSpec((None, 1, tf), lambda i, f: (layer, 0, nf + up_tile(f))),
            pl.BlockSpec((None, tf, d), lambda i, f: (layer, jnp.maximum(f - 1, 0), 0)),
            pl.BlockSpec((1, d), lambda i, f: (0, 0)),
        ],
        out_specs=pl.BlockSpec((tm, d), lambda i, f: (i, 0)),
        out_shape=jax.ShapeDtypeStruct((t, d), F32),
        scratch_shapes=[
            pltpu.VMEM((tm, d), BF16),
            pltpu.VMEM((nf, 2, FFN_TAIL, tf), F32),
            pltpu.VMEM((2, tm, tf), BF16),
        ],
        compiler_params=_params(("arbitrary", "arbitrary")),
        name=name,
    )(h, g, w_up, w_up, conv_w, conv_w, conv_b, conv_b, w_down, g_final)


def _rope(y, cos_t, sin_lo, sin_hi):
    outs = []
    for gidx in range(y.shape[1] // LANES):
        blk = y[:, gidx * LANES:(gidx + 1) * LANES]
        half = ROPE_DIM // 2
        outs.append(blk * cos_t + pltpu.roll(blk, half, 1) * sin_hi + pltpu.roll(blk, LANES - half, 1) * sin_lo)
    return jnp.concatenate(outs, axis=1)


def _kq_kernel(x_ref, pos_ref, invf_ref, gkv_ref, gq_ref, wk_ref, wq_ref, o_ref,
               xkv_ref, xq_ref, cos_ref, slo_ref, shi_ref, *, k_tiles):
    j = pl.program_id(1)

    @pl.when(j == 0)
    def _():
        y = _rms_scale(x_ref[...])
        xkv_ref[...] = (y * gkv_ref[...]).astype(BF16)
        xq_ref[...] = (y * gq_ref[...]).astype(BF16)
        half = ROPE_DIM // 2
        ang = invf_ref[...] * pos_ref[...].astype(F32)
        cos = jnp.cos(ang)
        sin = jnp.sin(ang)
        pad = LANES - ROPE_DIM
        tm = ang.shape[1]
        cos_ref[...] = jnp.concatenate([cos, cos, jnp.ones((pad, tm), F32)], axis=0).T
        sin_t = jnp.concatenate([-sin, sin, jnp.zeros((pad, tm), F32)], axis=0).T
        lane = lax.broadcasted_iota(jnp.int32, sin_t.shape, 1)
        slo_ref[...] = jnp.where(lane < half, sin_t, 0.0)
        shi_ref[...] = jnp.where(lane >= half, sin_t, 0.0)

    @pl.when(j < k_tiles)
    def _():
        y = jnp.dot(xkv_ref[...], wk_ref[...], preferred_element_type=F32)
        o_ref[...] = _rope(y, cos_ref[...], slo_ref[...], shi_ref[...]).astype(BF16)

    @pl.when(j >= k_tiles)
    def _():
        y = jnp.dot(xq_ref[...], wq_ref[...], preferred_element_type=F32)
        y = _rope(y, cos_ref[...], slo_ref[...], shi_ref[...]) * (A_QK_DIM ** -0.5)
        o_ref[...] = y.astype(BF16)


def _kq_proj(h, pos_rows, inv_freq_col, g_kv, g_q, w_kv, w_q, k_cols):
    t, d = h.shape
    tm, tn = PROJ_TM, KQ_TN
    k_tiles = k_cols // tn
    q_tiles = w_q.shape[1] // tn
    kern = functools.partial(_kq_kernel, k_tiles=k_tiles)
    return pl.pallas_call(
        kern,
        grid=(t // tm, k_tiles + q_tiles),
        in_specs=[
            pl.BlockSpec((tm, d), lambda i, j: (i, 0)),
            pl.BlockSpec((None, 1, tm), lambda i, j: (i, 0, 0)),
            pl.BlockSpec((ROPE_DIM // 2, 1), lambda i, j: (0, 0)),
            pl.BlockSpec((1, d), lambda i, j: (0, 0)),
            pl.BlockSpec((1, d), lambda i, j: (0, 0)),
            pl.BlockSpec((d, tn), lambda i, j: (0, jnp.minimum(j, k_tiles - 1))),
            pl.BlockSpec((d, tn), lambda i, j: (0, jnp.maximum(j - k_tiles, 0))),
        ],
        out_specs=pl.BlockSpec((tm, tn), lambda i, j: (i, j)),
        out_shape=jax.ShapeDtypeStruct((t, (k_tiles + q_tiles) * tn), BF16),
        scratch_shapes=[
            pltpu.VMEM((tm, d), BF16),
            pltpu.VMEM((tm, d), BF16),
            pltpu.VMEM((tm, LANES), F32),
            pltpu.VMEM((tm, LANES), F32),
            pltpu.VMEM((tm, LANES), F32),
        ],
        compiler_params=_params(("parallel", "arbitrary")),
        name="norm_kq_rope",
    )(h, pos_rows, inv_freq_col, g_kv, g_q, w_kv, w_q)


def _vt_kernel(x_ref, g_ref, wt_ref, o_ref, xn_ref):
    @pl.when(pl.program_id(1) == 0)
    def _():
        xn_ref[...] = (_rms_scale(x_ref[...]) * g_ref[...]).astype(BF16)

    o_ref[...] = lax.dot_general(wt_ref[...], xn_ref[...], NT_DIMS, preferred_element_type=F32).astype(BF16)


def _vt_proj(h, g, w_t):
    t, d = h.shape
    n = w_t.shape[0]
    tm, tn = PROJ_TM, VT_TN
    return pl.pallas_call(
        _vt_kernel,
        grid=(t // tm, n // tn),
        in_specs=[
            pl.BlockSpec((tm, d), lambda i, j: (i, 0)),
            pl.BlockSpec((1, d), lambda i, j: (0, 0)),
            pl.BlockSpec((tn, d), lambda i, j: (j, 0)),
        ],
        out_specs=pl.BlockSpec((tn, tm), lambda i, j: (j, i)),
        out_shape=jax.ShapeDtypeStruct((n, t), BF16),
        scratch_shapes=[pltpu.VMEM((tm, d), BF16)],
        compiler_params=_params(("parallel", "arbitrary")),
        name="norm_v_transposed",
    )(h, g, w_t)


def _diff_attn_kernel(k_ref, q_ref, vt_ref, lam_ref, g_ref, o_ref, *, lambda_init, seq):
    tb = ATT_T
    dk = A_QK_DIM
    nblk = seq // tb

    lv = lam_ref[...]
    lam = (jnp.exp(jnp.sum(lv[0:1, :] * lv[1:2, :], axis=1, keepdims=True))
           - jnp.exp(jnp.sum(lv[2:3, :] * lv[3:4, :], axis=1, keepdims=True)) + lambda_init)
    g_col = g_ref[...] * (1.0 - lambda_init)
    key_idx = lax.broadcasted_iota(jnp.int32, (tb, tb), 0)
    qry_idx = lax.broadcasted_iota(jnp.int32, (tb, tb), 1)
    diag_mask = key_idx <= qry_idx

    for qi in range(nblk):
        q = q_ref[qi * tb:(qi + 1) * tb, :]
        comps = []
        for c in range(2):
            qc = q[:, c * dk:(c + 1) * dk]
            m = l = acc = None
            for kb in range(qi + 1):
                kc = k_ref[kb * tb:(kb + 1) * tb, c * dk:(c + 1) * dk]
                vt = vt_ref[:, kb * tb:(kb + 1) * tb]
                s = lax.dot_general(kc, qc, NT_DIMS, preferred_element_type=F32)
                if kb == qi:
                    s = jnp.where(diag_mask, s, -jnp.inf)
                s_max = jnp.max(s, axis=0, keepdims=True)
                if kb == 0:
                    m = s_max
                    p = jnp.exp(s - m)
                    l = jnp.sum(p, axis=0, keepdims=True)
                    acc = jnp.dot(vt, p.astype(BF16), preferred_element_type=F32)
                else:
                    m_new = jnp.maximum(m, s_max)
                    alpha = jnp.exp(m - m_new)
                    p = jnp.exp(s - m_new)
                    l = alpha * l + jnp.sum(p, axis=0, keepdims=True)
                    acc = alpha * acc + jnp.dot(vt, p.astype(BF16), preferred_element_type=F32)
                    m = m_new
            comps.append(acc * (1.0 / l))
        o_t = comps[0] - lam * comps[1]
        o_t = o_t * lax.rsqrt(jnp.mean(o_t * o_t, axis=0, keepdims=True) + EPS) * g_col
        o_ref[qi * tb:(qi + 1) * tb, :] = o_t.T.astype(o_ref.dtype)


def _diff_attention(kq, v_t, lam_vecs, g_subln_col, batch, seq, lambda_init):
    hv = A_V_DIM
    return pl.pallas_call(
        functools.partial(_diff_attn_kernel, lambda_init=lambda_init, seq=seq),
        grid=(batch, A_HEADS),
        in_specs=[
            pl.BlockSpec((seq, hv), lambda b, h: (b, h)),
            pl.BlockSpec((seq, hv), lambda b, h: (b, A_HEADS + h)),
            pl.BlockSpec((hv, seq), lambda b, h: (h, b)),
            pl.BlockSpec((4, A_QK_DIM), lambda b, h: (0, 0)),
            pl.BlockSpec((hv, 1), lambda b, h: (0, 0)),
        ],
        out_specs=pl.BlockSpec((seq, hv), lambda b, h: (b, h)),
        out_shape=jax.ShapeDtypeStruct((batch * seq, A_HEADS * hv), BF16),
        compiler_params=_params(("parallel", "parallel")),
        name="diff_attention",
    )(kq, kq, v_t, lam_vecs, g_subln_col)


def kernel(x, positions, a_norm, m_w_in, m_b_igate, m_b_fgate, m_w_hnorm, m_w_out, kv_norm, w_kv, b_norm, w_q, lam_q1, lam_k1, lam_q2, lam_k2, subln, w_o, f_norm, w_up, conv_w, conv_b, w_down, final_norm):
    batch, seq, d = x.shape
    t = batch * seq
    depth = f_norm.shape[0]
    assert depth == 2 and a_norm.shape[0] == 1 and b_norm.shape[0] == 1
    assert seq % FFN_TM == 0 and seq % ATT_T == 0 and t % PROJ_TM == 0
    assert seq % MLSTM_L == 0 and 4 * (seq // MLSTM_L) <= MLSTM_L

    h = x.reshape(t, d)

    qkv_cols = 2 * M_HEADS * M_QK_DIM + M_HEADS * M_V_DIM
    o_cols = M_HEADS * M_V_DIM
    w_in = m_w_in[0].astype(BF16)
    w_gates = jnp.pad(m_w_in[0][:, qkv_cols + o_cols:], ((0, 0), (0, LANES - 2 * M_HEADS))).astype(BF16)
    qkv, o_gate, gates = _w_in_proj(h, a_norm[0][None, :], w_in, w_gates, qkv_cols, o_cols)

    nc = seq // MLSTM_L
    gates = gates[:, :2 * M_HEADS].reshape(batch, nc, MLSTM_L, 2, M_HEADS)
    gates_r = gates.transpose(0, 4, 3, 1, 2)
    bias = jnp.stack([m_b_igate[0], m_b_fgate[0]], axis=1)
    hg = _mlstm(qkv, o_gate, gates_r, bias[:, :, None], m_w_hnorm[0][:, None, :], batch, seq)
    h = _matmul_residual(hg, m_w_out[0].astype(BF16), h, PROJ_TM, RES_TN, "w_out_residual")

    f_gain = f_norm[:, None, :]
    w_up_b = w_up.astype(BF16)
    w_down_b = w_down.astype(BF16)
    conv_b3 = conv_b[:, None, :]
    h = _conv_ffn(h, 0, f_gain, w_up_b, conv_w, conv_b3, w_down_b, final_norm[None, :], seq, False, "conv_ffn_0")

    layer = 1
    lambda_init = 0.8 - 0.6 * math.exp(-0.3 * layer)
    half = ROPE_DIM // 2
    inv_freq_col = (ROPE_THETA ** (-jnp.arange(half, dtype=F32) / half))[:, None]
    k_cols = A_HEADS * 2 * A_QK_DIM
    w_kv_b = w_kv.astype(BF16)
    w_vt = w_kv[:, k_cols:].T.astype(BF16)
    pos_rows = positions.reshape(t // PROJ_TM, 1, PROJ_TM)
    kq = _kq_proj(h, pos_rows, inv_freq_col, kv_norm[None, :], b_norm[0][None, :], w_kv_b, w_q[0].astype(BF16), k_cols)
    v_t = _vt_proj(h, kv_norm[None, :], w_vt)
    lam_vecs = jnp.stack([lam_q1[0], lam_k1[0], lam_q2[0], lam_k2[0]], axis=0)
    att = _diff_attention(kq, v_t, lam_vecs, subln[0][:, None], batch, seq, lambda_init)
    h = _matmul_residual(att, w_o[0].astype(BF16), h, PROJ_TM, RES_TN, "w_o_residual")

    h = _conv_ffn(h, 1, f_gain, w_up_b, conv_w, conv_b3, w_down_b, final_norm[None, :], seq, True, "conv_ffn_1")
    return h.reshape(batch, seq, d)
```

```python
import functools
import math

import jax
import jax.numpy as jnp
from jax import lax
from jax.experimental import pallas as pl
from jax.experimental.pallas import tpu as pltpu

F32 = jnp.float32
BF16 = jnp.bfloat16

EPS = 1e-6
M_HEADS = 8
M_QK_DIM = 128
M_V_DIM = 256
GATE_SOFTCAP = 15.0
A_HEADS = 8
A_QK_DIM = 128
A_V_DIM = 256
ROPE_DIM = 32
ROPE_THETA = 500000.0
CONV_W = 3

LANES = 128
BF16_SUBLANES = 16
V7X_VMEM_LIMIT_BYTES = 56 * 1024 * 1024

PROJ_TM = 1024
W_IN_TN = 1024
KQ_TN = 1024
VT_TN = 512
RES_TM = 512
RES_TN = 512
FFN_TM = 1024
FFN_ROWS = 512
FFN_TF = 512
FFN_TAIL = 8
ATT_T = 256
MLSTM_L = LANES
MLSTM_HEADS = 2

NT_DIMS = (((1,), (1,)), ((), ()))


def _rms_scale(x):
    return x * lax.rsqrt(jnp.mean(x * x, axis=-1, keepdims=True) + EPS)


def _params(semantics):
    return pltpu.CompilerParams(dimension_semantics=semantics, vmem_limit_bytes=V7X_VMEM_LIMIT_BYTES)


def _w_in_kernel(x_ref, g_ref, w_ref, wg_ref, qkv_ref, o_ref, gates_ref, xn_ref, *, qkv_tiles):
    j = pl.program_id(1)

    @pl.when(j == 0)
    def _():
        xn = (_rms_scale(x_ref[...]) * g_ref[...]).astype(BF16)
        xn_ref[...] = xn
        gates_ref[...] = jnp.dot(xn, wg_ref[...], preferred_element_type=F32)
        y = jnp.dot(xn, w_ref[...], preferred_element_type=F32)
        qkv_ref[...] = (y * (M_QK_DIM ** -0.5)).astype(BF16)

    @pl.when((j > 0) & (j < qkv_tiles))
    def _():
        qkv_ref[...] = jnp.dot(xn_ref[...], w_ref[...], preferred_element_type=F32).astype(BF16)

    @pl.when(j >= qkv_tiles)
    def _():
        o_ref[...] = jnp.dot(xn_ref[...], w_ref[...], preferred_element_type=F32)


def _w_in_proj(x, g, w, w_gates, qkv_cols, o_cols):
    t, d = x.shape
    tm, tn = PROJ_TM, W_IN_TN
    assert M_HEADS * M_QK_DIM == tn and qkv_cols % tn == 0 and o_cols % tn == 0
    qkv_tiles = qkv_cols // tn
    return pl.pallas_call(
        functools.partial(_w_in_kernel, qkv_tiles=qkv_tiles),
        grid=(t // tm, (qkv_cols + o_cols) // tn),
        in_specs=[
            pl.BlockSpec((tm, d), lambda i, j: (i, 0)),
            pl.BlockSpec((1, d), lambda i, j: (0, 0)),
            pl.BlockSpec((d, tn), lambda i, j: (0, j)),
            pl.BlockSpec((d, LANES), lambda i, j: (0, 0)),
        ],
        out_specs=[
            pl.BlockSpec((tm, tn), lambda i, j: (i, jnp.minimum(j, qkv_tiles - 1))),
            pl.BlockSpec((tm, tn), lambda i, j: (i, jnp.maximum(j - qkv_tiles, 0))),
            pl.BlockSpec((tm, LANES), lambda i, j: (i, 0)),
        ],
        out_shape=[
            jax.ShapeDtypeStruct((t, qkv_cols), BF16),
            jax.ShapeDtypeStruct((t, o_cols), F32),
            jax.ShapeDtypeStruct((t, LANES), F32),
        ],
        scratch_shapes=[pltpu.VMEM((tm, d), BF16)],
        compiler_params=_params(("parallel", "arbitrary")),
        name="norm_w_in",
    )(x, g, w, w_gates)


def _softcap(t):
    return GATE_SOFTCAP * jnp.tanh(t / GATE_SOFTCAP)


def _log_sigmoid(x):
    return jnp.minimum(x, 0.0) - jnp.log1p(jnp.exp(-jnp.abs(x)))


def _lane_scan(x, op, fill):
    lane = lax.broadcasted_iota(jnp.int32, x.shape, 1)
    d = 1
    while d < x.shape[1]:
        x = op(x, jnp.where(lane >= d, pltpu.roll(x, d, 1), fill))
        d *= 2
    return x


def _mlstm_gate_tables(gr, bias, nc):
    L = MLSTM_L
    i_rows = _softcap(gr[0] + bias[0:1, :])
    f_rows = _log_sigmoid(_softcap(gr[1] + bias[1:2, :]))
    b_rows = _lane_scan(f_rows, jnp.add, 0.0)
    a_rows = i_rows - b_rows
    pm_rows = _lane_scan(a_rows, jnp.maximum, -jnp.inf)
    b_last = b_rows[:, L - 1:L]
    a_max = pm_rows[:, L - 1:L]

    m = jnp.zeros((1, 1), F32)
    ms = [m]
    for c in range(nc):
        m = b_last[c:c + 1, :] + jnp.maximum(m, a_max[c:c + 1, :])
        ms.append(m)
    m_in = jnp.concatenate(ms[:nc], axis=0)
    m_out = jnp.concatenate(ms[1:], axis=0)
    big_m_rows = jnp.maximum(m_in, pm_rows)
    decay = jnp.exp(b_last + m_in - m_out)

    stacked = jnp.concatenate([
        big_m_rows,
        jnp.exp(m_in - big_m_rows),
        jnp.exp(-(b_rows + big_m_rows)),
        jnp.exp(a_rows + b_last - m_out),
        jnp.zeros((L - 4 * nc, L), F32)], axis=0)
    return a_rows, stacked.T, decay


def _mlstm_kernel(q_ref, k_ref, v_ref, o_ref, gr_ref, br_ref, wn_ref, out_ref, *, seq):
    L = MLSTM_L
    nc = seq // L
    dk, dv = M_QK_DIM, M_V_DIM
    heads = range(MLSTM_HEADS)
    tables = [_mlstm_gate_tables(gr_ref[hh], br_ref[hh], nc) for hh in heads]

    row_t = lax.broadcasted_iota(jnp.int32, (L, L), 0)
    col_s = lax.broadcasted_iota(jnp.int32, (L, L), 1)
    causal = row_t >= col_s
    ones_blk = jnp.ones((L, LANES), BF16)

    states = [jnp.zeros((dk, dv + LANES), F32) for _ in heads]
    for c in range(nc):
        rows = slice(c * L, (c + 1) * L)
        for hh in heads:
            a_rows, cols, decay = tables[hh]
            qb = q_ref[rows, hh * dk:(hh + 1) * dk]
            kb = k_ref[rows, hh * dk:(hh + 1) * dk]
            v_aug = jnp.concatenate([v_ref[rows, hh * dv:(hh + 1) * dv], ones_blk], axis=1)
            big_m = cols[:, c:c + 1]
            w_inter = cols[:, nc + c:nc + c + 1]
            e_neg_m = cols[:, 2 * nc + c:2 * nc + c + 1]
            w_key = cols[:, 3 * nc + c:3 * nc + c + 1]

            e = jnp.exp(jnp.where(causal, a_rows[c:c + 1, :] - big_m, -jnp.inf))
            s = lax.dot_general(qb, kb, NT_DIMS, preferred_element_type=F32) * e
            nd = (w_inter * jnp.dot(qb, states[hh].astype(BF16), preferred_element_type=F32)
                  + jnp.dot(s.astype(BF16), v_aug, preferred_element_type=F32))
            den = nd[:, dv:]
            r = 1.0 / jnp.maximum(jnp.abs(den), e_neg_m)
            h = nd[:, :dv] * jnp.concatenate([r] * (dv // LANES), axis=1)

            hn = _rms_scale(h) * wn_ref[hh]
            gate = jax.nn.sigmoid(o_ref[rows, hh * dv:(hh + 1) * dv])
            out_ref[rows, hh * dv:(hh + 1) * dv] = (hn * gate).astype(out_ref.dtype)

            kw_t = (kb.astype(F32) * w_key).T.astype(BF16)
            states[hh] = decay[c:c + 1, :] * states[hh] + jnp.dot(kw_t, v_aug, preferred_element_type=F32)


def _mlstm(qkv, o_gate, gates_r, bias_r, w_hnorm, batch, seq):
    nc = seq // MLSTM_L
    g = MLSTM_HEADS
    qk_w, v_w = g * M_QK_DIM, g * M_V_DIM
    k_off = M_HEADS * M_QK_DIM // qk_w
    v_off = 2 * M_HEADS * M_QK_DIM // v_w
    return pl.pallas_call(
        functools.partial(_mlstm_kernel, seq=seq),
        grid=(batch, M_HEADS // g),
        in_specs=[
            pl.BlockSpec((seq, qk_w), lambda b, h: (b, h)),
            pl.BlockSpec((seq, qk_w), lambda b, h: (b, k_off + h)),
            pl.BlockSpec((seq, v_w), lambda b, h: (b, v_off + h)),
            pl.BlockSpec((seq, v_w), lambda b, h: (b, h)),
            pl.BlockSpec((None, g, 2, nc, MLSTM_L), lambda b, h: (b, h, 0, 0, 0)),
            pl.BlockSpec((g, 2, 1), lambda b, h: (h, 0, 0)),
            pl.BlockSpec((g, 1, M_V_DIM), lambda b, h: (h, 0, 0)),
        ],
        out_specs=pl.BlockSpec((seq, v_w), lambda b, h: (b, h)),
        out_shape=jax.ShapeDtypeStruct((batch * seq, M_HEADS * M_V_DIM), BF16),
        compiler_params=_params(("parallel", "parallel")),
        name="mlstm_chunkwise",
    )(qkv, qkv, qkv, o_gate, gates_r, bias_r, w_hnorm)


def _matmul_residual_kernel(a_ref, w_ref, r_ref, o_ref, *, tn):
    a = a_ref[...]
    for j in range(w_ref.shape[1] // tn):
        cols = slice(j * tn, (j + 1) * tn)
        o_ref[:, cols] = r_ref[:, cols] + jnp.dot(a, w_ref[:, cols], preferred_element_type=F32)


def _matmul_residual(a, w, res, tm, tn, name):
    t, kdim = a.shape
    n = w.shape[1]
    return pl.pallas_call(
        functools.partial(_matmul_residual_kernel, tn=tn),
        grid=(t // tm,),
        in_specs=[
            pl.BlockSpec((tm, kdim), lambda i: (i, 0)),
            pl.BlockSpec((kdim, n), lambda i: (0, 0)),
            pl.BlockSpec((tm, n), lambda i: (i, 0)),
        ],
        out_specs=pl.BlockSpec((tm, n), lambda i: (i, 0)),
        out_shape=jax.ShapeDtypeStruct((t, n), F32),
        compiler_params=_params(("parallel",)),
        name=name,
    )(a, w, res)


def _conv_ffn_kernel(h_ref, g_ref, wg_ref, wv_ref, cwg_ref, cwv_ref, cbg_ref, cbv_ref, wd_ref,
                     gf_ref, o_ref, xn_ref, tail_ref, *, tiles_per_seq, final_norm):
    i = pl.program_id(0)
    f = pl.program_id(1)
    tm = h_ref.shape[0]

    @pl.when(f == 0)
    def _():
        x = h_ref[...]
        xn_ref[...] = (_rms_scale(x) * g_ref[...]).astype(BF16)
        o_ref[...] = x

    @pl.when(i % tiles_per_seq == 0)
    def _():
        tail_ref[f] = jnp.zeros(tail_ref.shape[1:], F32)

    history = [tail_ref[f, 0], tail_ref[f, 1]]
    for r in range(tm // FFN_ROWS):
        rows = slice(r * FFN_ROWS, (r + 1) * FFN_ROWS)
        xn = xn_ref[rows, :]

        def conv(w_ref, cw_ref, cb_ref, slot):
            u = jnp.dot(xn, w_ref[...], preferred_element_type=F32)
            ue = jnp.concatenate([history[slot], u], axis=0)
            history[slot] = u[FFN_ROWS - FFN_TAIL:, :]
            cw = cw_ref[...]
            c = cb_ref[...] + pltpu.roll(ue, 2, 0) * cw[0:1, :] + pltpu.roll(ue, 1, 0) * cw[1:2, :] + ue * cw[2:3, :]
            return c[FFN_TAIL:, :]

        gate = conv(wg_ref, cwg_ref, cbg_ref, 0)
        val = conv(wv_ref, cwv_ref, cbv_ref, 1)
        act = (gate * jax.nn.sigmoid(gate) * val).astype(BF16)
        o_ref[rows, :] += jnp.dot(act, wd_ref[...], preferred_element_type=F32)
    tail_ref[f, 0] = history[0]
    tail_ref[f, 1] = history[1]

    if final_norm:
        @pl.when(f == pl.num_programs(1) - 1)
        def _():
            o_ref[...] = _rms_scale(o_ref[...]) * gf_ref[...]


def _conv_ffn(h, layer, g, w_up, conv_w, conv_b, w_down, g_final, seq, final_norm, name):
    t, d = h.shape
    d_ff = w_down.shape[1]
    tm, tf = FFN_TM, FFN_TF
    nf = d_ff // tf
    kern = functools.partial(_conv_ffn_kernel, tiles_per_seq=seq // tm, final_norm=final_norm)
    return pl.pallas_call(
        kern,
        grid=(t // tm, nf),
        in_specs=[
            pl.BlockSpec((tm, d), lambda i, f: (i, 0)),
            pl.BlockSpec((None, 1, d), lambda i, f: (layer, 0, 0)),
            pl.BlockSpec((None, d, tf), lambda i, f: (layer, 0, f)),
            pl.BlockSpec((None, d, tf), lambda i, f: (layer, 0, nf + f)),
            pl.BlockSpec((None, CONV_W, tf), lambda i, f: (layer, 0, f)),
            pl.BlockSpec((None, CONV_W, tf), lambda i, f: (layer, 0, nf + f)),
            pl.BlockSpec((None, 1, tf), lambda i, f: (layer, 0, f)),
            pl.BlockSpec((None, 1, tf), lambda i, f: (layer, 0, nf + f)),
            pl.BlockSpec((None, tf, d), lambda i, f: (layer, f, 0)),
            pl.BlockSpec((1, d), lambda i, f: (0, 0)),
        ],
        out_specs=pl.BlockSpec((tm, d), lambda i, f: (i, 0)),
        out_shape=jax.ShapeDtypeStruct((t, d), F32),
        scratch_shapes=[
            pltpu.VMEM((tm, d), BF16),
            pltpu.VMEM((nf, 2, FFN_TAIL, tf), F32),
        ],
        compiler_params=_params(("arbitrary", "arbitrary")),
        name=name,
    )(h, g, w_up, w_up, conv_w, conv_w, conv_b, conv_b, w_down, g_final)


def _rope(y, cos_t, sin_lo, sin_hi):
    outs = []
    for gidx in range(y.shape[1] // LANES):
        blk = y[:, gidx * LANES:(gidx + 1) * LANES]
        half = ROPE_DIM // 2
        outs.append(blk * cos_t + pltpu.roll(blk, half, 1) * sin_hi + pltpu.roll(blk, LANES - half, 1) * sin_lo)
    return jnp.concatenate(outs, axis=1)


def _kq_kernel(x_ref, pos_ref, invf_ref, gkv_ref, gq_ref, wk_ref, wq_ref, o_ref,
               xkv_ref, xq_ref, cos_ref, slo_ref, shi_ref, *, k_tiles):
    j = pl.program_id(1)

    @pl.when(j == 0)
    def _():
        y = _rms_scale(x_ref[...])
        xkv_ref[...] = (y * gkv_ref[...]).astype(BF16)
        xq_ref[...] = (y * gq_ref[...]).astype(BF16)
        half = ROPE_DIM // 2
        ang = invf_ref[...] * pos_ref[...].astype(F32)
        cos = jnp.cos(ang)
        sin = jnp.sin(ang)
        pad = LANES - ROPE_DIM
        tm = ang.shape[1]
        cos_ref[...] = jnp.concatenate([cos, cos, jnp.ones((pad, tm), F32)], axis=0).T
        sin_t = jnp.concatenate([-sin, sin, jnp.zeros((pad, tm), F32)], axis=0).T
        lane = lax.broadcasted_iota(jnp.int32, sin_t.shape, 1)
        slo_ref[...] = jnp.where(lane < half, sin_t, 0.0)
        shi_ref[...] = jnp.where(lane >= half, sin_t, 0.0)

    @pl.when(j < k_tiles)
    def _():
        y = jnp.dot(xkv_ref[...], wk_ref[...], preferred_element_type=F32)
        o_ref[...] = _rope(y, cos_ref[...], slo_ref[...], shi_ref[...]).astype(BF16)

    @pl.when(j >= k_tiles)
    def _():
        y = jnp.dot(xq_ref[...], wq_ref[...], preferred_element_type=F32)
        y = _rope(y, cos_ref[...], slo_ref[...], shi_ref[...]) * (A_QK_DIM ** -0.5)
        o_ref[...] = y.astype(BF16)


def _kq_proj(h, pos_rows, inv_freq_col, g_kv, g_q, w_kv, w_q, k_cols):
    t, d = h.shape
    tm, tn = PROJ_TM, KQ_TN
    k_tiles = k_cols // tn
    q_tiles = w_q.shape[1] // tn
    kern = functools.partial(_kq_kernel, k_tiles=k_tiles)
    return pl.pallas_call(
        kern,
        grid=(t // tm, k_tiles + q_tiles),
        in_specs=[
            pl.BlockSpec((tm, d), lambda i, j: (i, 0)),
            pl.BlockSpec((None, 1, tm), lambda i, j: (i, 0, 0)),
            pl.BlockSpec((ROPE_DIM // 2, 1), lambda i, j: (0, 0)),
            pl.BlockSpec((1, d), lambda i, j: (0, 0)),
            pl.BlockSpec((1, d), lambda i, j: (0, 0)),
            pl.BlockSpec((d, tn), lambda i, j: (0, jnp.minimum(j, k_tiles - 1))),
            pl.BlockSpec((d, tn), lambda i, j: (0, jnp.maximum(j - k_tiles, 0))),
        ],
        out_specs=pl.BlockSpec((tm, tn), lambda i, j: (i, j)),
        out_shape=jax.ShapeDtypeStruct((t, (k_tiles + q_tiles) * tn), BF16),
        scratch_shapes=[
            pltpu.VMEM((tm, d), BF16),
            pltpu.VMEM((tm, d), BF16),
            pltpu.VMEM((tm, LANES), F32),
            pltpu.VMEM((tm, LANES), F32),
            pltpu.VMEM((tm, LANES), F32),
        ],
        compiler_params=_params(("parallel", "arbitrary")),
        name="norm_kq_rope",
    )(h, pos_rows, inv_freq_col, g_kv, g_q, w_kv, w_q)


def _vt_kernel(x_ref, g_ref, wt_ref, o_ref, xn_ref):
    @pl.when(pl.program_id(1) == 0)
    def _():
        xn_ref[...] = (_rms_scale(x_ref[...]) * g_ref[...]).astype(BF16)

    o_ref[...] = lax.dot_general(wt_ref[...], xn_ref[...], NT_DIMS, preferred_element_type=F32).astype(BF16)


def _vt_proj(h, g, w_t):
    t, d = h.shape
    n = w_t.shape[0]
    tm, tn = PROJ_TM, VT_TN
    return pl.pallas_call(
        _vt_kernel,
        grid=(t // tm, n // tn),
        in_specs=[
            pl.BlockSpec((tm, d), lambda i, j: (i, 0)),
            pl.BlockSpec((1, d), lambda i, j: (0, 0)),
            pl.BlockSpec((tn, d), lambda i, j: (j, 0)),
        ],
        out_specs=pl.BlockSpec((tn, tm), lambda i, j: (j, i)),
        out_shape=jax.ShapeDtypeStruct((n, t), BF16),
        scratch_shapes=[pltpu.VMEM((tm, d), BF16)],
        compiler_params=_params(("parallel", "arbitrary")),
        name="norm_v_transposed",
    )(h, g, w_t)


def _diff_attn_kernel(k_ref, q_ref, vt_ref, lam_ref, g_ref, o_ref, *, lambda_init, seq):
    tb = ATT_T
    dk = A_QK_DIM
    nblk = seq // tb

    lv = lam_ref[...]
    lam = (jnp.exp(jnp.sum(lv[0:1, :] * lv[1:2, :], axis=1, keepdims=True))
           - jnp.exp(jnp.sum(lv[2:3, :] * lv[3:4, :], axis=1, keepdims=True)) + lambda_init)
    g_col = g_ref[...] * (1.0 - lambda_init)
    key_idx = lax.broadcasted_iota(jnp.int32, (tb, tb), 0)
    qry_idx = lax.broadcasted_iota(jnp.int32, (tb, tb), 1)
    diag_mask = key_idx <= qry_idx

    for qi in range(nblk):
        q = q_ref[qi * tb:(qi + 1) * tb, :]
        comps = []
        for c in range(2):
            qc = q[:, c * dk:(c + 1) * dk]
            m = l = acc = None
            for kb in range(qi + 1):
                kc = k_ref[kb * tb:(kb + 1) * tb, c * dk:(c + 1) * dk]
                vt = vt_ref[:, kb * tb:(kb + 1) * tb]
                s = lax.dot_general(kc, qc, NT_DIMS, preferred_element_type=F32)
                if kb == qi:
                    s = jnp.where(diag_mask, s, -jnp.inf)
                s_max = jnp.max(s, axis=0, keepdims=True)
                if kb == 0:
                    m = s_max
                    p = jnp.exp(s - m)
                    l = jnp.sum(p, axis=0, keepdims=True)
                    acc = jnp.dot(vt, p.astype(BF16), preferred_element_type=F32)
                else:
                    m_new = jnp.maximum(m, s_max)
                    alpha = jnp.exp(m - m_new)
                    p = jnp.exp(s - m_new)
                    l = alpha * l + jnp.sum(p, axis=0, keepdims=True)
                    acc = alpha * acc + jnp.dot(vt, p.astype(BF16), preferred_element_type=F32)
                    m = m_new
            comps.append(acc * (1.0 / l))
        o_t = comps[0] - lam * comps[1]
        o_t = o_t * lax.rsqrt(jnp.mean(o_t * o_t, axis=0, keepdims=True) + EPS) * g_col
        o_ref[qi * tb:(qi + 1) * tb, :] = o_t.T.astype(o_ref.dtype)


def _diff_attention(kq, v_t, lam_vecs, g_subln_col, batch, seq, lambda_init):
    hv = A_V_DIM
    return pl.pallas_call(
        functools.partial(_diff_attn_kernel, lambda_init=lambda_init, seq=seq),
        grid=(batch, A_HEADS),
        in_specs=[
            pl.BlockSpec((seq, hv), lambda b, h: (b, h)),
            pl.BlockSpec((seq, hv), lambda b, h: (b, A_HEADS + h)),
            pl.BlockSpec((hv, seq), lambda b, h: (h, b)),
            pl.BlockSpec((4, A_QK_DIM), lambda b, h: (0, 0)),
            pl.BlockSpec((hv, 1), lambda b, h: (0, 0)),
        ],
        out_specs=pl.BlockSpec((seq, hv), lambda b, h: (b, h)),
        out_shape=jax.ShapeDtypeStruct((batch * seq, A_HEADS * hv), BF16),
        compiler_params=_params(("parallel", "parallel")),
        name="diff_attention",
    )(kq, kq, v_t, lam_vecs, g_subln_col)


def kernel(x, positions, a_norm, m_w_in, m_b_igate, m_b_fgate, m_w_hnorm, m_w_out, kv_norm, w_kv, b_norm, w_q, lam_q1, lam_k1, lam_q2, lam_k2, subln, w_o, f_norm, w_up, conv_w, conv_b, w_down, final_norm):
    batch, seq, d = x.shape
    t = batch * seq
    depth = f_norm.shape[0]
    assert depth == 2 and a_norm.shape[0] == 1 and b_norm.shape[0] == 1
    assert seq % FFN_TM == 0 and seq % ATT_T == 0 and t % PROJ_TM == 0
    assert seq % MLSTM_L == 0 and 4 * (seq // MLSTM_L) <= MLSTM_L

    h = x.reshape(t, d)

    qkv_cols = 2 * M_HEADS * M_QK_DIM + M_HEADS * M_V_DIM
    o_cols = M_HEADS * M_V_DIM
    w_in = m_w_in[0].astype(BF16)
    w_gates = jnp.pad(m_w_in[0][:, qkv_cols + o_cols:], ((0, 0), (0, LANES - 2 * M_HEADS))).astype(BF16)
    qkv, o_gate, gates = _w_in_proj(h, a_norm[0][None, :], w_in, w_gates, qkv_cols, o_cols)

    nc = seq // MLSTM_L
    gates = gates[:, :2 * M_HEADS].reshape(batch, nc, MLSTM_L, 2, M_HEADS)
    gates_r = gates.transpose(0, 4, 3, 1, 2)
    bias = jnp.stack([m_b_igate[0], m_b_fgate[0]], axis=1)
    hg = _mlstm(qkv, o_gate, gates_r, bias[:, :, None], m_w_hnorm[0][:, None, :], batch, seq)
    h = _matmul_residual(hg, m_w_out[0].astype(BF16), h, RES_TM, RES_TN, "w_out_residual")

    f_gain = f_norm[:, None, :]
    w_up_b = w_up.astype(BF16)
    w_down_b = w_down.astype(BF16)
    conv_b3 = conv_b[:, None, :]
    h = _conv_ffn(h, 0, f_gain, w_up_b, conv_w, conv_b3, w_down_b, final_norm[None, :], seq, False, "conv_ffn_0")

    layer = 1
    lambda_init = 0.8 - 0.6 * math.exp(-0.3 * layer)
    half = ROPE_DIM // 2
    inv_freq_col = (ROPE_THETA ** (-jnp.arange(half, dtype=F32) / half))[:, None]
    k_cols = A_HEADS * 2 * A_QK_DIM
    w_kv_b = w_kv.astype(BF16)
    w_vt = w_kv[:, k_cols:].T.astype(BF16)
    pos_rows = positions.reshape(t // PROJ_TM, 1, PROJ_TM)
    kq = _kq_proj(h, pos_rows, inv_freq_col, kv_norm[None, :], b_norm[0][None, :], w_kv_b, w_q[0].astype(BF16), k_cols)
    v_t = _vt_proj(h, kv_norm[None, :], w_vt)
    lam_vecs = jnp.stack([lam_q1[0], lam_k1[0], lam_q2[0], lam_k2[0]], axis=0)
    att = _diff_attention(kq, v_t, lam_vecs, subln[0][:, None], batch, seq, lambda_init)
    h = _matmul_residual(att, w_o[0].astype(BF16), h, RES_TM, RES_TN, "w_o_residual")

    h = _conv_ffn(h, 1, f_gain, w_up_b, conv_w, conv_b3, w_down_b, final_norm[None, :], seq, True, "conv_ffn_1")
    return h.reshape(batch, seq, d)
```

```python
import functools
import math

import jax
import jax.numpy as jnp
from jax import lax
from jax.experimental import pallas as pl
from jax.experimental.pallas import tpu as pltpu

F32 = jnp.float32
BF16 = jnp.bfloat16

EPS = 1e-6
M_HEADS = 8
M_QK_DIM = 128
M_V_DIM = 256
GATE_SOFTCAP = 15.0
A_HEADS = 8
A_QK_DIM = 128
A_V_DIM = 256
ROPE_DIM = 32
ROPE_THETA = 500000.0
CONV_W = 3

LANES = 128
V7X_VMEM_LIMIT_BYTES = 56 * 1024 * 1024

PROJ_TM = 1024
W_IN_TN = 1024
KVQ_TN = 1024
RES_TM = 512
RES_TN = 512
FFN_TM = 1024
FFN_ROWS = 512
FFN_TF = 512
FFN_TAIL = 8
ATT_T = 256
MLSTM_L = LANES
MLSTM_HEADS = 2

NT_DIMS = (((1,), (1,)), ((), ()))
Q_SCALE_LOG2 = (A_QK_DIM ** -0.5) * math.log2(math.e)


def _rms_scale(x):
    return x * lax.rsqrt(jnp.mean(x * x, axis=-1, keepdims=True) + EPS)


def _params(semantics):
    return pltpu.CompilerParams(dimension_semantics=semantics, vmem_limit_bytes=V7X_VMEM_LIMIT_BYTES)


def _w_in_kernel(x_ref, g_ref, w_ref, wg_ref, qkv_ref, o_ref, gates_ref, xn_ref, *, qkv_tiles):
    j = pl.program_id(1)

    @pl.when(j == 0)
    def _():
        xn = (_rms_scale(x_ref[...]) * g_ref[...]).astype(BF16)
        xn_ref[...] = xn
        gates_ref[...] = jnp.dot(xn, wg_ref[...], preferred_element_type=F32)
        y = jnp.dot(xn, w_ref[...], preferred_element_type=F32)
        qkv_ref[...] = (y * (M_QK_DIM ** -0.5)).astype(BF16)

    @pl.when((j > 0) & (j < qkv_tiles))
    def _():
        qkv_ref[...] = jnp.dot(xn_ref[...], w_ref[...], preferred_element_type=F32).astype(BF16)

    @pl.when(j >= qkv_tiles)
    def _():
        o_ref[...] = jnp.dot(xn_ref[...], w_ref[...], preferred_element_type=F32)


def _w_in_proj(x, g, w, w_gates, qkv_cols, o_cols):
    t, d = x.shape
    tm, tn = PROJ_TM, W_IN_TN
    assert M_HEADS * M_QK_DIM == tn and qkv_cols % tn == 0 and o_cols % tn == 0
    qkv_tiles = qkv_cols // tn
    return pl.pallas_call(
        functools.partial(_w_in_kernel, qkv_tiles=qkv_tiles),
        grid=(t // tm, (qkv_cols + o_cols) // tn),
        in_specs=[
            pl.BlockSpec((tm, d), lambda i, j: (i, 0)),
            pl.BlockSpec((1, d), lambda i, j: (0, 0)),
            pl.BlockSpec((d, tn), lambda i, j: (0, j)),
            pl.BlockSpec((d, LANES), lambda i, j: (0, 0)),
        ],
        out_specs=[
            pl.BlockSpec((tm, tn), lambda i, j: (i, jnp.minimum(j, qkv_tiles - 1))),
            pl.BlockSpec((tm, tn), lambda i, j: (i, jnp.maximum(j - qkv_tiles, 0))),
            pl.BlockSpec((tm, LANES), lambda i, j: (i, 0)),
        ],
        out_shape=[
            jax.ShapeDtypeStruct((t, qkv_cols), BF16),
            jax.ShapeDtypeStruct((t, o_cols), F32),
            jax.ShapeDtypeStruct((t, LANES), F32),
        ],
        scratch_shapes=[pltpu.VMEM((tm, d), BF16)],
        compiler_params=_params(("parallel", "arbitrary")),
        name="norm_w_in",
    )(x, g, w, w_gates)


def _softcap(t):
    return GATE_SOFTCAP * jnp.tanh(t / GATE_SOFTCAP)


def _log_sigmoid(x):
    return jnp.minimum(x, 0.0) - jnp.log1p(jnp.exp(-jnp.abs(x)))


def _lane_scan(x, op, fill):
    lane = lax.broadcasted_iota(jnp.int32, x.shape, 1)
    d = 1
    while d < x.shape[1]:
        x = op(x, jnp.where(lane >= d, pltpu.roll(x, d, 1), fill))
        d *= 2
    return x


def _mlstm_gate_tables(gr, bias, nc):
    L = MLSTM_L
    i_rows = _softcap(gr[0] + bias[0:1, :])
    f_rows = _log_sigmoid(_softcap(gr[1] + bias[1:2, :]))
    b_rows = _lane_scan(f_rows, jnp.add, 0.0)
    a_rows = i_rows - b_rows
    pm_rows = _lane_scan(a_rows, jnp.maximum, -jnp.inf)
    b_last = b_rows[:, L - 1:L]
    a_max = pm_rows[:, L - 1:L]

    m = jnp.zeros((1, 1), F32)
    ms = [m]
    for c in range(nc):
        m = b_last[c:c + 1, :] + jnp.maximum(m, a_max[c:c + 1, :])
        ms.append(m)
    m_in = jnp.concatenate(ms[:nc], axis=0)
    m_out = jnp.concatenate(ms[1:], axis=0)
    big_m_rows = jnp.maximum(m_in, pm_rows)
    decay = jnp.exp(b_last + m_in - m_out)

    stacked = jnp.concatenate([
        big_m_rows,
        jnp.exp(m_in - big_m_rows),
        jnp.exp(-(b_rows + big_m_rows)),
        jnp.exp(a_rows + b_last - m_out),
        jnp.zeros((L - 4 * nc, L), F32)], axis=0)
    return a_rows, stacked.T, decay


def _mlstm_kernel(q_ref, k_ref, v_ref, o_ref, gr_ref, br_ref, wn_ref, out_ref, *, seq):
    L = MLSTM_L
    nc = seq // L
    dk, dv = M_QK_DIM, M_V_DIM
    heads = range(MLSTM_HEADS)
    tables = [_mlstm_gate_tables(gr_ref[hh], br_ref[hh], nc) for hh in heads]

    row_t = lax.broadcasted_iota(jnp.int32, (L, L), 0)
    col_s = lax.broadcasted_iota(jnp.int32, (L, L), 1)
    causal = row_t >= col_s
    ones_blk = jnp.ones((L, LANES), BF16)

    states = [jnp.zeros((dk, dv + LANES), F32) for _ in heads]
    for c in range(nc):
        rows = slice(c * L, (c + 1) * L)
        for hh in heads:
            a_rows, cols, decay = tables[hh]
            qb = q_ref[rows, hh * dk:(hh + 1) * dk]
            kb = k_ref[rows, hh * dk:(hh + 1) * dk]
            v_aug = jnp.concatenate([v_ref[rows, hh * dv:(hh + 1) * dv], ones_blk], axis=1)
            big_m = cols[:, c:c + 1]
            w_inter = cols[:, nc + c:nc + c + 1]
            e_neg_m = cols[:, 2 * nc + c:2 * nc + c + 1]
            w_key = cols[:, 3 * nc + c:3 * nc + c + 1]

            e = jnp.exp(jnp.where(causal, a_rows[c:c + 1, :] - big_m, -jnp.inf))
            s = lax.dot_general(qb, kb, NT_DIMS, preferred_element_type=F32) * e
            nd = (w_inter * jnp.dot(qb, states[hh].astype(BF16), preferred_element_type=F32)
                  + jnp.dot(s.astype(BF16), v_aug, preferred_element_type=F32))
            den = nd[:, dv:]
            r = 1.0 / jnp.maximum(jnp.abs(den), e_neg_m)
            h = nd[:, :dv] * jnp.concatenate([r] * (dv // LANES), axis=1)

            hn = _rms_scale(h) * wn_ref[hh]
            gate = jax.nn.sigmoid(o_ref[rows, hh * dv:(hh + 1) * dv])
            out_ref[rows, hh * dv:(hh + 1) * dv] = (hn * gate).astype(out_ref.dtype)

            kw_t = (kb.astype(F32) * w_key).T.astype(BF16)
            states[hh] = decay[c:c + 1, :] * states[hh] + jnp.dot(kw_t, v_aug, preferred_element_type=F32)


def _mlstm(qkv, o_gate, gates_r, bias_r, w_hnorm, batch, seq):
    nc = seq // MLSTM_L
    g = MLSTM_HEADS
    qk_w, v_w = g * M_QK_DIM, g * M_V_DIM
    k_off = M_HEADS * M_QK_DIM // qk_w
    v_off = 2 * M_HEADS * M_QK_DIM // v_w
    return pl.pallas_call(
        functools.partial(_mlstm_kernel, seq=seq),
        grid=(batch, M_HEADS // g),
        in_specs=[
            pl.BlockSpec((seq, qk_w), lambda b, h: (b, h)),
            pl.BlockSpec((seq, qk_w), lambda b, h: (b, k_off + h)),
            pl.BlockSpec((seq, v_w), lambda b, h: (b, v_off + h)),
            pl.BlockSpec((seq, v_w), lambda b, h: (b, h)),
            pl.BlockSpec((None, g, 2, nc, MLSTM_L), lambda b, h: (b, h, 0, 0, 0)),
            pl.BlockSpec((g, 2, 1), lambda b, h: (h, 0, 0)),
            pl.BlockSpec((g, 1, M_V_DIM), lambda b, h: (h, 0, 0)),
        ],
        out_specs=pl.BlockSpec((seq, v_w), lambda b, h: (b, h)),
        out_shape=jax.ShapeDtypeStruct((batch * seq, M_HEADS * M_V_DIM), BF16),
        compiler_params=_params(("parallel", "parallel")),
        name="mlstm_chunkwise",
    )(qkv, qkv, qkv, o_gate, gates_r, bias_r, w_hnorm)


def _matmul_residual_kernel(a_ref, w_ref, r_ref, o_ref, *, tn):
    a = a_ref[...]
    for j in range(w_ref.shape[1] // tn):
        cols = slice(j * tn, (j + 1) * tn)
        o_ref[:, cols] = r_ref[:, cols] + jnp.dot(a, w_ref[:, cols], preferred_element_type=F32)


def _matmul_residual(a, w, res, tm, tn, name):
    t, kdim = a.shape
    n = w.shape[1]
    return pl.pallas_call(
        functools.partial(_matmul_residual_kernel, tn=tn),
        grid=(t // tm,),
        in_specs=[
            pl.BlockSpec((tm, kdim), lambda i: (i, 0)),
            pl.BlockSpec((kdim, n), lambda i: (0, 0)),
            pl.BlockSpec((tm, n), lambda i: (i, 0)),
        ],
        out_specs=pl.BlockSpec((tm, n), lambda i: (i, 0)),
        out_shape=jax.ShapeDtypeStruct((t, n), F32),
        compiler_params=_params(("parallel",)),
        name=name,
    )(a, w, res)


def _conv_ffn_kernel(h_ref, g_ref, wg_ref, wv_ref, cwg_ref, cwv_ref, cbg_ref, cbv_ref, wd_ref,
                     gf_ref, o_ref, xn_ref, tail_ref, *, tiles_per_seq, final_norm):
    i = pl.program_id(0)
    f = pl.program_id(1)
    tm = h_ref.shape[0]

    @pl.when(f == 0)
    def _():
        x = h_ref[...]
        xn_ref[...] = (_rms_scale(x) * g_ref[...]).astype(BF16)
        o_ref[...] = x

    @pl.when(i % tiles_per_seq == 0)
    def _():
        tail_ref[f] = jnp.zeros(tail_ref.shape[1:], F32)

    history = [tail_ref[f, 0], tail_ref[f, 1]]
    for r in range(tm // FFN_ROWS):
        rows = slice(r * FFN_ROWS, (r + 1) * FFN_ROWS)
        xn = xn_ref[rows, :]

        def conv(w_ref, cw_ref, cb_ref, slot):
            u = jnp.dot(xn, w_ref[...], preferred_element_type=F32)
            ue = jnp.concatenate([history[slot], u], axis=0)
            history[slot] = u[FFN_ROWS - FFN_TAIL:, :]
            cw = cw_ref[...]
            c = cb_ref[...] + pltpu.roll(ue, 2, 0) * cw[0:1, :] + pltpu.roll(ue, 1, 0) * cw[1:2, :] + ue * cw[2:3, :]
            return c[FFN_TAIL:, :]

        gate = conv(wg_ref, cwg_ref, cbg_ref, 0)
        val = conv(wv_ref, cwv_ref, cbv_ref, 1)
        act = (gate * jax.nn.sigmoid(gate) * val).astype(BF16)
        o_ref[rows, :] += jnp.dot(act, wd_ref[...], preferred_element_type=F32)
    tail_ref[f, 0] = history[0]
    tail_ref[f, 1] = history[1]

    if final_norm:
        @pl.when(f == pl.num_programs(1) - 1)
        def _():
            o_ref[...] = _rms_scale(o_ref[...]) * gf_ref[...]


def _conv_ffn(h, layer, g, w_up, conv_w, conv_b, w_down, g_final, seq, final_norm, name):
    t, d = h.shape
    d_ff = w_down.shape[1]
    tm, tf = FFN_TM, FFN_TF
    nf = d_ff // tf
    kern = functools.partial(_conv_ffn_kernel, tiles_per_seq=seq // tm, final_norm=final_norm)
    return pl.pallas_call(
        kern,
        grid=(t // tm, nf),
        in_specs=[
            pl.BlockSpec((tm, d), lambda i, f: (i, 0)),
            pl.BlockSpec((None, 1, d), lambda i, f: (layer, 0, 0)),
            pl.BlockSpec((None, d, tf), lambda i, f: (layer, 0, f)),
            pl.BlockSpec((None, d, tf), lambda i, f: (layer, 0, nf + f)),
            pl.BlockSpec((None, CONV_W, tf), lambda i, f: (layer, 0, f)),
            pl.BlockSpec((None, CONV_W, tf), lambda i, f: (layer, 0, nf + f)),
            pl.BlockSpec((None, 1, tf), lambda i, f: (layer, 0, f)),
            pl.BlockSpec((None, 1, tf), lambda i, f: (layer, 0, nf + f)),
            pl.BlockSpec((None, tf, d), lambda i, f: (layer, f, 0)),
            pl.BlockSpec((1, d), lambda i, f: (0, 0)),
        ],
        out_specs=pl.BlockSpec((tm, d), lambda i, f: (i, 0)),
        out_shape=jax.ShapeDtypeStruct((t, d), F32),
        scratch_shapes=[
            pltpu.VMEM((tm, d), BF16),
            pltpu.VMEM((nf, 2, FFN_TAIL, tf), F32),
        ],
        compiler_params=_params(("arbitrary", "arbitrary")),
        name=name,
    )(h, g, w_up, w_up, conv_w, conv_w, conv_b, conv_b, w_down, g_final)


def _rope(y, cos_t, sin_lo, sin_hi):
    outs = []
    for gidx in range(y.shape[1] // LANES):
        blk = y[:, gidx * LANES:(gidx + 1) * LANES]
        half = ROPE_DIM // 2
        outs.append(blk * cos_t + pltpu.roll(blk, half, 1) * sin_hi + pltpu.roll(blk, LANES - half, 1) * sin_lo)
    return jnp.concatenate(outs, axis=1)


def _kvq_kernel(x_ref, pos_ref, invf_ref, gkv_ref, gq_ref, wkv_ref, wq_ref, o_ref,
                xkv_ref, xq_ref, cos_ref, slo_ref, shi_ref, *, k_tiles, kv_tiles):
    j = pl.program_id(1)

    @pl.when(j == 0)
    def _():
        y = _rms_scale(x_ref[...])
        xkv_ref[...] = (y * gkv_ref[...]).astype(BF16)
        xq_ref[...] = (y * gq_ref[...]).astype(BF16)
        half = ROPE_DIM // 2
        ang = invf_ref[...] * pos_ref[...].astype(F32)
        cos = jnp.cos(ang)
        sin = jnp.sin(ang)
        pad = LANES - ROPE_DIM
        tm = ang.shape[1]
        cos_ref[...] = jnp.concatenate([cos, cos, jnp.ones((pad, tm), F32)], axis=0).T
        sin_t = jnp.concatenate([-sin, sin, jnp.zeros((pad, tm), F32)], axis=0).T
        lane = lax.broadcasted_iota(jnp.int32, sin_t.shape, 1)
        slo_ref[...] = jnp.where(lane < half, sin_t, 0.0)
        shi_ref[...] = jnp.where(lane >= half, sin_t, 0.0)

    @pl.when(j < k_tiles)
    def _():
        y = jnp.dot(xkv_ref[...], wkv_ref[...], preferred_element_type=F32)
        o_ref[...] = _rope(y, cos_ref[...], slo_ref[...], shi_ref[...]).astype(BF16)

    @pl.when((j >= k_tiles) & (j < kv_tiles))
    def _():
        o_ref[...] = jnp.dot(xkv_ref[...], wkv_ref[...], preferred_element_type=F32).astype(BF16)

    @pl.when(j >= kv_tiles)
    def _():
        y = jnp.dot(xq_ref[...], wq_ref[...], preferred_element_type=F32)
        y = _rope(y, cos_ref[...], slo_ref[...], shi_ref[...]) * Q_SCALE_LOG2
        o_ref[...] = y.astype(BF16)


def _kvq_proj(h, pos_rows, inv_freq_col, g_kv, g_q, w_kv, w_q, k_cols):
    t, d = h.shape
    tm, tn = PROJ_TM, KVQ_TN
    k_tiles = k_cols // tn
    kv_tiles = w_kv.shape[1] // tn
    q_tiles = w_q.shape[1] // tn
    kern = functools.partial(_kvq_kernel, k_tiles=k_tiles, kv_tiles=kv_tiles)
    return pl.pallas_call(
        kern,
        grid=(t // tm, kv_tiles + q_tiles),
        in_specs=[
            pl.BlockSpec((tm, d), lambda i, j: (i, 0)),
            pl.BlockSpec((None, 1, tm), lambda i, j: (i, 0, 0)),
            pl.BlockSpec((ROPE_DIM // 2, 1), lambda i, j: (0, 0)),
            pl.BlockSpec((1, d), lambda i, j: (0, 0)),
            pl.BlockSpec((1, d), lambda i, j: (0, 0)),
            pl.BlockSpec((d, tn), lambda i, j: (0, jnp.minimum(j, kv_tiles - 1))),
            pl.BlockSpec((d, tn), lambda i, j: (0, jnp.maximum(j - kv_tiles, 0))),
        ],
        out_specs=pl.BlockSpec((tm, tn), lambda i, j: (i, j)),
        out_shape=jax.ShapeDtypeStruct((t, (kv_tiles + q_tiles) * tn), BF16),
        scratch_shapes=[
            pltpu.VMEM((tm, d), BF16),
            pltpu.VMEM((tm, d), BF16),
            pltpu.VMEM((tm, LANES), F32),
            pltpu.VMEM((tm, LANES), F32),
            pltpu.VMEM((tm, LANES), F32),
        ],
        compiler_params=_params(("parallel", "arbitrary")),
        name="norm_kvq_rope",
    )(h, pos_rows, inv_freq_col, g_kv, g_q, w_kv, w_q)


def _diff_attn_kernel(k_ref, v_ref, q_ref, lam_ref, g_ref, o_ref, *, lambda_init, seq):
    tb = ATT_T
    dk = A_QK_DIM
    nblk = seq // tb

    lv = lam_ref[...]
    lam = (jnp.exp(jnp.sum(lv[0:1, :] * lv[1:2, :], axis=1, keepdims=True))
           - jnp.exp(jnp.sum(lv[2:3, :] * lv[3:4, :], axis=1, keepdims=True)) + lambda_init)
    g_col = g_ref[...] * (1.0 - lambda_init)
    key_idx = lax.broadcasted_iota(jnp.int32, (tb, tb), 0)
    qry_idx = lax.broadcasted_iota(jnp.int32, (tb, tb), 1)
    diag_mask = key_idx <= qry_idx
    v_t = [v_ref[kb * tb:(kb + 1) * tb, :].astype(F32).T.astype(BF16) for kb in range(nblk)]

    for qi in range(nblk):
        q = q_ref[qi * tb:(qi + 1) * tb, :]
        comps = []
        for c in range(2):
            qc = q[:, c * dk:(c + 1) * dk]
            m = l = acc = None
            for kb in range(qi + 1):
                kc = k_ref[kb * tb:(kb + 1) * tb, c * dk:(c + 1) * dk]
                s = lax.dot_general(kc, qc, NT_DIMS, preferred_element_type=F32)
                if kb == qi:
                    s = jnp.where(diag_mask, s, -jnp.inf)
                s_max = jnp.max(s, axis=0, keepdims=True)
                if kb == 0:
                    m = s_max
                    p = jnp.exp2(s - m)
                    l = jnp.sum(p, axis=0, keepdims=True)
                    acc = jnp.dot(v_t[kb], p.astype(BF16), preferred_element_type=F32)
                else:
                    m_new = jnp.maximum(m, s_max)
                    alpha = jnp.exp2(m - m_new)
                    p = jnp.exp2(s - m_new)
                    l = alpha * l + jnp.sum(p, axis=0, keepdims=True)
                    acc = alpha * acc + jnp.dot(v_t[kb], p.astype(BF16), preferred_element_type=F32)
                    m = m_new
            comps.append(acc * (1.0 / l))
        o_t = comps[0] - lam * comps[1]
        o_t = o_t * lax.rsqrt(jnp.mean(o_t * o_t, axis=0, keepdims=True) + EPS) * g_col
        o_ref[qi * tb:(qi + 1) * tb, :] = o_t.T.astype(o_ref.dtype)


def _diff_attention(kvq, lam_vecs, g_subln_col, batch, seq, lambda_init):
    hv = A_V_DIM
    return pl.pallas_call(
        functools.partial(_diff_attn_kernel, lambda_init=lambda_init, seq=seq),
        grid=(batch, A_HEADS),
        in_specs=[
            pl.BlockSpec((seq, hv), lambda b, h: (b, h)),
            pl.BlockSpec((seq, hv), lambda b, h: (b, A_HEADS + h)),
            pl.BlockSpec((seq, hv), lambda b, h: (b, 2 * A_HEADS + h)),
            pl.BlockSpec((4, A_QK_DIM), lambda b, h: (0, 0)),
            pl.BlockSpec((hv, 1), lambda b, h: (0, 0)),
        ],
        out_specs=pl.BlockSpec((seq, hv), lambda b, h: (b, h)),
        out_shape=jax.ShapeDtypeStruct((batch * seq, A_HEADS * hv), BF16),
        compiler_params=_params(("parallel", "parallel")),
        name="diff_attention",
    )(kvq, kvq, kvq, lam_vecs, g_subln_col)


def kernel(x, positions, a_norm, m_w_in, m_b_igate, m_b_fgate, m_w_hnorm, m_w_out, kv_norm, w_kv, b_norm, w_q, lam_q1, lam_k1, lam_q2, lam_k2, subln, w_o, f_norm, w_up, conv_w, conv_b, w_down, final_norm):
    batch, seq, d = x.shape
    t = batch * seq
    depth = f_norm.shape[0]
    assert depth == 2 and a_norm.shape[0] == 1 and b_norm.shape[0] == 1
    assert seq % FFN_TM == 0 and seq % ATT_T == 0 and t % PROJ_TM == 0
    assert seq % MLSTM_L == 0 and 4 * (seq // MLSTM_L) <= MLSTM_L

    h = x.reshape(t, d)

    qkv_cols = 2 * M_HEADS * M_QK_DIM + M_HEADS * M_V_DIM
    o_cols = M_HEADS * M_V_DIM
    w_in = m_w_in[0].astype(BF16)
    w_gates = jnp.pad(m_w_in[0][:, qkv_cols + o_cols:], ((0, 0), (0, LANES - 2 * M_HEADS))).astype(BF16)
    qkv, o_gate, gates = _w_in_proj(h, a_norm[0][None, :], w_in, w_gates, qkv_cols, o_cols)

    nc = seq // MLSTM_L
    gates = gates[:, :2 * M_HEADS].reshape(batch, nc, MLSTM_L, 2, M_HEADS)
    gates_r = gates.transpose(0, 4, 3, 1, 2)
    bias = jnp.stack([m_b_igate[0], m_b_fgate[0]], axis=1)
    hg = _mlstm(qkv, o_gate, gates_r, bias[:, :, None], m_w_hnorm[0][:, None, :], batch, seq)
    h = _matmul_residual(hg, m_w_out[0].astype(BF16), h, RES_TM, RES_TN, "w_out_residual")

    f_gain = f_norm[:, None, :]
    w_up_b = w_up.astype(BF16)
    w_down_b = w_down.astype(BF16)
    conv_b3 = conv_b[:, None, :]
    h = _conv_ffn(h, 0, f_gain, w_up_b, conv_w, conv_b3, w_down_b, final_norm[None, :], seq, False, "conv_ffn_0")

    layer = 1
    lambda_init = 0.8 - 0.6 * math.exp(-0.3 * layer)
    half = ROPE_DIM // 2
    inv_freq_col = (ROPE_THETA ** (-jnp.arange(half, dtype=F32) / half))[:, None]
    k_cols = A_HEADS * 2 * A_QK_DIM
    pos_rows = positions.reshape(t // PROJ_TM, 1, PROJ_TM)
    kvq = _kvq_proj(h, pos_rows, inv_freq_col, kv_norm[None, :], b_norm[0][None, :],
                    w_kv.astype(BF16), w_q[0].astype(BF16), k_cols)
    lam_vecs = jnp.stack([lam_q1[0], lam_k1[0], lam_q2[0], lam_k2[0]], axis=0)
    att = _diff_attention(kvq, lam_vecs, subln[0][:, None], batch, seq, lambda_init)
    h = _matmul_residual(att, w_o[0].astype(BF16), h, RES_TM, RES_TN, "w_o_residual")

    h = _conv_ffn(h, 1, f_gain, w_up_b, conv_w, conv_b3, w_down_b, final_norm[None, :], seq, True, "conv_ffn_1")
    return h.reshape(batch, seq, d)
```

```python
import functools
import math

import jax
import jax.numpy as jnp
from jax import lax
from jax.experimental import pallas as pl
from jax.experimental.pallas import tpu as pltpu

F32 = jnp.float32
BF16 = jnp.bfloat16

EPS = 1e-6
M_HEADS = 8
M_QK_DIM = 128
M_V_DIM = 256
GATE_SOFTCAP = 15.0
A_HEADS = 8
A_QK_DIM = 128
A_V_DIM = 256
ROPE_DIM = 32
ROPE_THETA = 500000.0
CONV_W = 3

LANES = 128
V7X_VMEM_LIMIT_BYTES = 56 * 1024 * 1024

PROJ_TM = 1024
W_IN_TN = 1024
KVQ_TN = 1024
RES_TM = 512
RES_TN = 512
FFN_TM = 1024
FFN_ROWS = 512
FFN_TF = 512
FFN_TAIL = 8
ATT_T = 256
MLSTM_L = LANES
MLSTM_HEADS = 2

NT_DIMS = (((1,), (1,)), ((), ()))
Q_SCALE_LOG2 = (A_QK_DIM ** -0.5) * math.log2(math.e)


def _rms_scale(x):
    return x * lax.rsqrt(jnp.mean(x * x, axis=-1, keepdims=True) + EPS)


def _params(semantics):
    return pltpu.CompilerParams(dimension_semantics=semantics, vmem_limit_bytes=V7X_VMEM_LIMIT_BYTES)


def _cast_specs(weights, layer, steps, step_of):
    in_specs, out_specs, out_shapes = [], [], []
    for w in weights:
        rows, cols = w.shape[1:]
        assert rows % steps == 0
        blk = rows // steps
        in_specs.append(pl.BlockSpec((None, blk, cols), lambda *idx: (layer, step_of(*idx), 0)))
        out_specs.append(pl.BlockSpec((blk, cols), lambda *idx: (step_of(*idx), 0)))
        out_shapes.append(jax.ShapeDtypeStruct((rows, cols), BF16))
    return in_specs, out_specs, out_shapes


def _cast_blocks(src_refs, dst_refs):
    for src, dst in zip(src_refs, dst_refs):
        dst[...] = src[...].astype(BF16)


def _w_in_kernel(x_ref, g_ref, w_ref, wg_ref, qkv_ref, o_ref, gates_ref, xn_ref, *, qkv_tiles):
    j = pl.program_id(1)

    @pl.when(j == 0)
    def _():
        xn = (_rms_scale(x_ref[...]) * g_ref[...]).astype(BF16)
        xn_ref[...] = xn
        gates_ref[...] = jnp.dot(xn, wg_ref[...], preferred_element_type=F32)
        y = jnp.dot(xn, w_ref[...], preferred_element_type=F32)
        qkv_ref[...] = (y * (M_QK_DIM ** -0.5)).astype(BF16)

    @pl.when((j > 0) & (j < qkv_tiles))
    def _():
        qkv_ref[...] = jnp.dot(xn_ref[...], w_ref[...], preferred_element_type=F32).astype(BF16)

    @pl.when(j >= qkv_tiles)
    def _():
        o_ref[...] = jnp.dot(xn_ref[...], w_ref[...], preferred_element_type=F32)


def _w_in_proj(x, g, w, w_gates, qkv_cols, o_cols):
    t, d = x.shape
    tm, tn = PROJ_TM, W_IN_TN
    assert M_HEADS * M_QK_DIM == tn and qkv_cols % tn == 0 and o_cols % tn == 0
    qkv_tiles = qkv_cols // tn
    return pl.pallas_call(
        functools.partial(_w_in_kernel, qkv_tiles=qkv_tiles),
        grid=(t // tm, (qkv_cols + o_cols) // tn),
        in_specs=[
            pl.BlockSpec((tm, d), lambda i, j: (i, 0)),
            pl.BlockSpec((1, d), lambda i, j: (0, 0)),
            pl.BlockSpec((d, tn), lambda i, j: (0, j)),
            pl.BlockSpec((d, LANES), lambda i, j: (0, 0)),
        ],
        out_specs=[
            pl.BlockSpec((tm, tn), lambda i, j: (i, jnp.minimum(j, qkv_tiles - 1))),
            pl.BlockSpec((tm, tn), lambda i, j: (i, jnp.maximum(j - qkv_tiles, 0))),
            pl.BlockSpec((tm, LANES), lambda i, j: (i, 0)),
        ],
        out_shape=[
            jax.ShapeDtypeStruct((t, qkv_cols), BF16),
            jax.ShapeDtypeStruct((t, o_cols), F32),
            jax.ShapeDtypeStruct((t, LANES), F32),
        ],
        scratch_shapes=[pltpu.VMEM((tm, d), BF16)],
        compiler_params=_params(("parallel", "arbitrary")),
        name="norm_w_in",
    )(x, g, w, w_gates)


def _softcap(t):
    return GATE_SOFTCAP * jnp.tanh(t / GATE_SOFTCAP)


def _log_sigmoid(x):
    return jnp.minimum(x, 0.0) - jnp.log1p(jnp.exp(-jnp.abs(x)))


def _lane_scan(x, op, fill):
    lane = lax.broadcasted_iota(jnp.int32, x.shape, 1)
    d = 1
    while d < x.shape[1]:
        x = op(x, jnp.where(lane >= d, pltpu.roll(x, d, 1), fill))
        d *= 2
    return x


def _mlstm_gate_tables(gr, bias, nc):
    L = MLSTM_L
    i_rows = _softcap(gr[0] + bias[0:1, :])
    f_rows = _log_sigmoid(_softcap(gr[1] + bias[1:2, :]))
    b_rows = _lane_scan(f_rows, jnp.add, 0.0)
    a_rows = i_rows - b_rows
    pm_rows = _lane_scan(a_rows, jnp.maximum, -jnp.inf)
    b_last = b_rows[:, L - 1:L]
    a_max = pm_rows[:, L - 1:L]

    m = jnp.zeros((1, 1), F32)
    ms = [m]
    for c in range(nc):
        m = b_last[c:c + 1, :] + jnp.maximum(m, a_max[c:c + 1, :])
        ms.append(m)
    m_in = jnp.concatenate(ms[:nc], axis=0)
    m_out = jnp.concatenate(ms[1:], axis=0)
    big_m_rows = jnp.maximum(m_in, pm_rows)
    decay = jnp.exp(b_last + m_in - m_out)

    stacked = jnp.concatenate([
        big_m_rows,
        jnp.exp(m_in - big_m_rows),
        jnp.exp(-(b_rows + big_m_rows)),
        jnp.exp(a_rows + b_last - m_out),
        jnp.zeros((L - 4 * nc, L), F32)], axis=0)
    return a_rows, stacked.T, decay


def _mlstm_kernel(q_ref, k_ref, v_ref, o_ref, gr_ref, br_ref, wn_ref, wa_ref, wb_ref,
                  out_ref, wa_out_ref, wb_out_ref, *, seq):
    L = MLSTM_L
    nc = seq // L
    dk, dv = M_QK_DIM, M_V_DIM
    heads = range(MLSTM_HEADS)
    _cast_blocks((wa_ref, wb_ref), (wa_out_ref, wb_out_ref))
    tables = [_mlstm_gate_tables(gr_ref[hh], br_ref[hh], nc) for hh in heads]

    row_t = lax.broadcasted_iota(jnp.int32, (L, L), 0)
    col_s = lax.broadcasted_iota(jnp.int32, (L, L), 1)
    causal = row_t >= col_s
    ones_blk = jnp.ones((L, LANES), BF16)

    states = [jnp.zeros((dk, dv + LANES), F32) for _ in heads]
    for c in range(nc):
        rows = slice(c * L, (c + 1) * L)
        for hh in heads:
            a_rows, cols, decay = tables[hh]
            qb = q_ref[rows, hh * dk:(hh + 1) * dk]
            kb = k_ref[rows, hh * dk:(hh + 1) * dk]
            v_aug = jnp.concatenate([v_ref[rows, hh * dv:(hh + 1) * dv], ones_blk], axis=1)
            big_m = cols[:, c:c + 1]
            w_inter = cols[:, nc + c:nc + c + 1]
            e_neg_m = cols[:, 2 * nc + c:2 * nc + c + 1]
            w_key = cols[:, 3 * nc + c:3 * nc + c + 1]

            e = jnp.exp(jnp.where(causal, a_rows[c:c + 1, :] - big_m, -jnp.inf))
            s = lax.dot_general(qb, kb, NT_DIMS, preferred_element_type=F32) * e
            nd = (w_inter * jnp.dot(qb, states[hh].astype(BF16), preferred_element_type=F32)
                  + jnp.dot(s.astype(BF16), v_aug, preferred_element_type=F32))
            den = nd[:, dv:]
            r = 1.0 / jnp.maximum(jnp.abs(den), e_neg_m)
            h = nd[:, :dv] * jnp.concatenate([r] * (dv // LANES), axis=1)

            hn = _rms_scale(h) * wn_ref[hh]
            gate = jax.nn.sigmoid(o_ref[rows, hh * dv:(hh + 1) * dv])
            out_ref[rows, hh * dv:(hh + 1) * dv] = (hn * gate).astype(out_ref.dtype)

            kw_t = (kb.astype(F32) * w_key).T.astype(BF16)
            states[hh] = decay[c:c + 1, :] * states[hh] + jnp.dot(kw_t, v_aug, preferred_element_type=F32)


def _mlstm(qkv, o_gate, gates_r, bias_r, w_hnorm, cast_weights, cast_layer, batch, seq):
    nc = seq // MLSTM_L
    g = MLSTM_HEADS
    groups = M_HEADS // g
    cast_in, cast_out, cast_shapes = _cast_specs(cast_weights, cast_layer, batch * groups,
                                                 lambda b, h: b * groups + h)
    qk_w, v_w = g * M_QK_DIM, g * M_V_DIM
    k_off = M_HEADS * M_QK_DIM // qk_w
    v_off = 2 * M_HEADS * M_QK_DIM // v_w
    return pl.pallas_call(
        functools.partial(_mlstm_kernel, seq=seq),
        grid=(batch, groups),
        in_specs=[
            pl.BlockSpec((seq, qk_w), lambda b, h: (b, h)),
            pl.BlockSpec((seq, qk_w), lambda b, h: (b, k_off + h)),
            pl.BlockSpec((seq, v_w), lambda b, h: (b, v_off + h)),
            pl.BlockSpec((seq, v_w), lambda b, h: (b, h)),
            pl.BlockSpec((None, g, 2, nc, MLSTM_L), lambda b, h: (b, h, 0, 0, 0)),
            pl.BlockSpec((g, 2, 1), lambda b, h: (h, 0, 0)),
            pl.BlockSpec((g, 1, M_V_DIM), lambda b, h: (h, 0, 0)),
        ] + cast_in,
        out_specs=[pl.BlockSpec((seq, v_w), lambda b, h: (b, h))] + cast_out,
        out_shape=[jax.ShapeDtypeStruct((batch * seq, M_HEADS * M_V_DIM), BF16)] + cast_shapes,
        compiler_params=_params(("parallel", "parallel")),
        name="mlstm_chunkwise",
    )(qkv, qkv, qkv, o_gate, gates_r, bias_r, w_hnorm, *cast_weights)


def _matmul_residual_kernel(a_ref, w_ref, r_ref, o_ref, *, tn):
    a = a_ref[...]
    for j in range(w_ref.shape[1] // tn):
        cols = slice(j * tn, (j + 1) * tn)
        o_ref[:, cols] = r_ref[:, cols] + jnp.dot(a, w_ref[:, cols], preferred_element_type=F32)


def _matmul_residual(a, w, res, tm, tn, name):
    t, kdim = a.shape
    n = w.shape[1]
    return pl.pallas_call(
        functools.partial(_matmul_residual_kernel, tn=tn),
        grid=(t // tm,),
        in_specs=[
            pl.BlockSpec((tm, kdim), lambda i: (i, 0)),
            pl.BlockSpec((kdim, n), lambda i: (0, 0)),
            pl.BlockSpec((tm, n), lambda i: (i, 0)),
        ],
        out_specs=pl.BlockSpec((tm, n), lambda i: (i, 0)),
        out_shape=jax.ShapeDtypeStruct((t, n), F32),
        compiler_params=_params(("parallel",)),
        name=name,
    )(a, w, res)


def _conv_ffn_kernel(h_ref, g_ref, wg_ref, wv_ref, cwg_ref, cwv_ref, cbg_ref, cbv_ref, wd_ref,
                     gf_ref, o_ref, xn_ref, tail_ref, *, tiles_per_seq, final_norm):
    i = pl.program_id(0)
    f = pl.program_id(1)
    tm = h_ref.shape[0]

    @pl.when(f == 0)
    def _():
        x = h_ref[...]
        xn_ref[...] = (_rms_scale(x) * g_ref[...]).astype(BF16)
        o_ref[...] = x

    @pl.when(i % tiles_per_seq == 0)
    def _():
        tail_ref[f] = jnp.zeros(tail_ref.shape[1:], F32)

    history = [tail_ref[f, 0], tail_ref[f, 1]]
    for r in range(tm // FFN_ROWS):
        rows = slice(r * FFN_ROWS, (r + 1) * FFN_ROWS)
        xn = xn_ref[rows, :]

        def conv(w_ref, cw_ref, cb_ref, slot):
            u = jnp.dot(xn, w_ref[...], preferred_element_type=F32)
            ue = jnp.concatenate([history[slot], u], axis=0)
            history[slot] = u[FFN_ROWS - FFN_TAIL:, :]
            cw = cw_ref[...]
            c = cb_ref[...] + pltpu.roll(ue, 2, 0) * cw[0:1, :] + pltpu.roll(ue, 1, 0) * cw[1:2, :] + ue * cw[2:3, :]
            return c[FFN_TAIL:, :]

        gate = conv(wg_ref, cwg_ref, cbg_ref, 0)
        val = conv(wv_ref, cwv_ref, cbv_ref, 1)
        act = (gate * jax.nn.sigmoid(gate) * val).astype(BF16)
        o_ref[rows, :] += jnp.dot(act, wd_ref[...], preferred_element_type=F32)
    tail_ref[f, 0] = history[0]
    tail_ref[f, 1] = history[1]

    if final_norm:
        @pl.when(f == pl.num_programs(1) - 1)
        def _():
            o_ref[...] = _rms_scale(o_ref[...]) * gf_ref[...]


def _conv_ffn(h, layer, g, w_up, conv_w, conv_b, w_down, g_final, seq, final_norm, name):
    t, d = h.shape
    d_ff = w_down.shape[0]
    tm, tf = FFN_TM, FFN_TF
    nf = d_ff // tf
    kern = functools.partial(_conv_ffn_kernel, tiles_per_seq=seq // tm, final_norm=final_norm)
    return pl.pallas_call(
        kern,
        grid=(t // tm, nf),
        in_specs=[
            pl.BlockSpec((tm, d), lambda i, f: (i, 0)),
            pl.BlockSpec((None, 1, d), lambda i, f: (layer, 0, 0)),
            pl.BlockSpec((d, tf), lambda i, f: (0, f)),
            pl.BlockSpec((d, tf), lambda i, f: (0, nf + f)),
            pl.BlockSpec((None, CONV_W, tf), lambda i, f: (layer, 0, f)),
            pl.BlockSpec((None, CONV_W, tf), lambda i, f: (layer, 0, nf + f)),
            pl.BlockSpec((None, 1, tf), lambda i, f: (layer, 0, f)),
            pl.BlockSpec((None, 1, tf), lambda i, f: (layer, 0, nf + f)),
            pl.BlockSpec((tf, d), lambda i, f: (f, 0)),
            pl.BlockSpec((1, d), lambda i, f: (0, 0)),
        ],
        out_specs=pl.BlockSpec((tm, d), lambda i, f: (i, 0)),
        out_shape=jax.ShapeDtypeStruct((t, d), F32),
        scratch_shapes=[
            pltpu.VMEM((tm, d), BF16),
            pltpu.VMEM((nf, 2, FFN_TAIL, tf), F32),
        ],
        compiler_params=_params(("arbitrary", "arbitrary")),
        name=name,
    )(h, g, w_up, w_up, conv_w, conv_w, conv_b, conv_b, w_down, g_final)


def _rope(y, cos_t, sin_lo, sin_hi):
    outs = []
    for gidx in range(y.shape[1] // LANES):
        blk = y[:, gidx * LANES:(gidx + 1) * LANES]
        half = ROPE_DIM // 2
        outs.append(blk * cos_t + pltpu.roll(blk, half, 1) * sin_hi + pltpu.roll(blk, LANES - half, 1) * sin_lo)
    return jnp.concatenate(outs, axis=1)


def _kvq_kernel(x_ref, pos_ref, invf_ref, gkv_ref, gq_ref, wkv_ref, wq_ref, o_ref,
                xkv_ref, xq_ref, cos_ref, slo_ref, shi_ref, *, k_tiles, kv_tiles):
    j = pl.program_id(1)

    @pl.when(j == 0)
    def _():
        y = _rms_scale(x_ref[...])
        xkv_ref[...] = (y * gkv_ref[...]).astype(BF16)
        xq_ref[...] = (y * gq_ref[...]).astype(BF16)
        half = ROPE_DIM // 2
        ang = invf_ref[...] * pos_ref[...].astype(F32)
        cos = jnp.cos(ang)
        sin = jnp.sin(ang)
        pad = LANES - ROPE_DIM
        tm = ang.shape[1]
        cos_ref[...] = jnp.concatenate([cos, cos, jnp.ones((pad, tm), F32)], axis=0).T
        sin_t = jnp.concatenate([-sin, sin, jnp.zeros((pad, tm), F32)], axis=0).T
        lane = lax.broadcasted_iota(jnp.int32, sin_t.shape, 1)
        slo_ref[...] = jnp.where(lane < half, sin_t, 0.0)
        shi_ref[...] = jnp.where(lane >= half, sin_t, 0.0)

    @pl.when(j < k_tiles)
    def _():
        y = jnp.dot(xkv_ref[...], wkv_ref[...], preferred_element_type=F32)
        o_ref[...] = _rope(y, cos_ref[...], slo_ref[...], shi_ref[...]).astype(BF16)

    @pl.when((j >= k_tiles) & (j < kv_tiles))
    def _():
        o_ref[...] = jnp.dot(xkv_ref[...], wkv_ref[...], preferred_element_type=F32).astype(BF16)

    @pl.when(j >= kv_tiles)
    def _():
        y = jnp.dot(xq_ref[...], wq_ref[...], preferred_element_type=F32)
        y = _rope(y, cos_ref[...], slo_ref[...], shi_ref[...]) * Q_SCALE_LOG2
        o_ref[...] = y.astype(BF16)


def _kvq_proj(h, pos_rows, inv_freq_col, g_kv, g_q, w_kv, w_q, k_cols):
    t, d = h.shape
    tm, tn = PROJ_TM, KVQ_TN
    k_tiles = k_cols // tn
    kv_tiles = w_kv.shape[1] // tn
    q_tiles = w_q.shape[1] // tn
    kern = functools.partial(_kvq_kernel, k_tiles=k_tiles, kv_tiles=kv_tiles)
    return pl.pallas_call(
        kern,
        grid=(t // tm, kv_tiles + q_tiles),
        in_specs=[
            pl.BlockSpec((tm, d), lambda i, j: (i, 0)),
            pl.BlockSpec((None, 1, tm), lambda i, j: (i, 0, 0)),
            pl.BlockSpec((ROPE_DIM // 2, 1), lambda i, j: (0, 0)),
            pl.BlockSpec((1, d), lambda i, j: (0, 0)),
            pl.BlockSpec((1, d), lambda i, j: (0, 0)),
            pl.BlockSpec((d, tn), lambda i, j: (0, jnp.minimum(j, kv_tiles - 1))),
            pl.BlockSpec((d, tn), lambda i, j: (0, jnp.maximum(j - kv_tiles, 0))),
        ],
        out_specs=pl.BlockSpec((tm, tn), lambda i, j: (i, j)),
        out_shape=jax.ShapeDtypeStruct((t, (kv_tiles + q_tiles) * tn), BF16),
        scratch_shapes=[
            pltpu.VMEM((tm, d), BF16),
            pltpu.VMEM((tm, d), BF16),
            pltpu.VMEM((tm, LANES), F32),
            pltpu.VMEM((tm, LANES), F32),
            pltpu.VMEM((tm, LANES), F32),
        ],
        compiler_params=_params(("parallel", "arbitrary")),
        name="norm_kvq_rope",
    )(h, pos_rows, inv_freq_col, g_kv, g_q, w_kv, w_q)


def _diff_attn_kernel(k_ref, v_ref, q_ref, lam_ref, g_ref, wa_ref, wb_ref, o_ref, wa_out_ref, wb_out_ref,
                      *, lambda_init, seq):
    tb = ATT_T
    _cast_blocks((wa_ref, wb_ref), (wa_out_ref, wb_out_ref))
    dk = A_QK_DIM
    nblk = seq // tb

    lv = lam_ref[...]
    lam = (jnp.exp(jnp.sum(lv[0:1, :] * lv[1:2, :], axis=1, keepdims=True))
           - jnp.exp(jnp.sum(lv[2:3, :] * lv[3:4, :], axis=1, keepdims=True)) + lambda_init)
    g_col = g_ref[...] * (1.0 - lambda_init)
    key_idx = lax.broadcasted_iota(jnp.int32, (tb, tb), 0)
    qry_idx = lax.broadcasted_iota(jnp.int32, (tb, tb), 1)
    diag_mask = key_idx <= qry_idx
    v_t = [v_ref[kb * tb:(kb + 1) * tb, :].astype(F32).T.astype(BF16) for kb in range(nblk)]

    for qi in range(nblk):
        q = q_ref[qi * tb:(qi + 1) * tb, :]
        comps = []
        for c in range(2):
            qc = q[:, c * dk:(c + 1) * dk]
            m = l = acc = None
            for kb in range(qi + 1):
                kc = k_ref[kb * tb:(kb + 1) * tb, c * dk:(c + 1) * dk]
                s = lax.dot_general(kc, qc, NT_DIMS, preferred_element_type=F32)
                if kb == qi:
                    s = jnp.where(diag_mask, s, -jnp.inf)
                s_max = jnp.max(s, axis=0, keepdims=True)
                if kb == 0:
                    m = s_max
                    p = jnp.exp2(s - m)
                    l = jnp.sum(p, axis=0, keepdims=True)
                    acc = jnp.dot(v_t[kb], p.astype(BF16), preferred_element_type=F32)
                else:
                    m_new = jnp.maximum(m, s_max)
                    alpha = jnp.exp2(m - m_new)
                    p = jnp.exp2(s - m_new)
                    l = alpha * l + jnp.sum(p, axis=0, keepdims=True)
                    acc = alpha * acc + jnp.dot(v_t[kb], p.astype(BF16), preferred_element_type=F32)
                    m = m_new
            comps.append(acc * (1.0 / l))
        o_t = comps[0] - lam * comps[1]
        o_t = o_t * lax.rsqrt(jnp.mean(o_t * o_t, axis=0, keepdims=True) + EPS) * g_col
        o_ref[qi * tb:(qi + 1) * tb, :] = o_t.T.astype(o_ref.dtype)


def _diff_attention(kvq, lam_vecs, g_subln_col, cast_weights, cast_layer, batch, seq, lambda_init):
    hv = A_V_DIM
    cast_in, cast_out, cast_shapes = _cast_specs(cast_weights, cast_layer, batch * A_HEADS,
                                                 lambda b, h: b * A_HEADS + h)
    return pl.pallas_call(
        functools.partial(_diff_attn_kernel, lambda_init=lambda_init, seq=seq),
        grid=(batch, A_HEADS),
        in_specs=[
            pl.BlockSpec((seq, hv), lambda b, h: (b, h)),
            pl.BlockSpec((seq, hv), lambda b, h: (b, A_HEADS + h)),
            pl.BlockSpec((seq, hv), lambda b, h: (b, 2 * A_HEADS + h)),
            pl.BlockSpec((4, A_QK_DIM), lambda b, h: (0, 0)),
            pl.BlockSpec((hv, 1), lambda b, h: (0, 0)),
        ] + cast_in,
        out_specs=[pl.BlockSpec((seq, hv), lambda b, h: (b, h))] + cast_out,
        out_shape=[jax.ShapeDtypeStruct((batch * seq, A_HEADS * hv), BF16)] + cast_shapes,
        compiler_params=_params(("parallel", "parallel")),
        name="diff_attention",
    )(kvq, kvq, kvq, lam_vecs, g_subln_col, *cast_weights)


def kernel(x, positions, a_norm, m_w_in, m_b_igate, m_b_fgate, m_w_hnorm, m_w_out, kv_norm, w_kv, b_norm, w_q, lam_q1, lam_k1, lam_q2, lam_k2, subln, w_o, f_norm, w_up, conv_w, conv_b, w_down, final_norm):
    batch, seq, d = x.shape
    t = batch * seq
    depth = f_norm.shape[0]
    assert depth == 2 and a_norm.shape[0] == 1 and b_norm.shape[0] == 1
    assert seq % FFN_TM == 0 and seq % ATT_T == 0 and t % PROJ_TM == 0
    assert seq % MLSTM_L == 0 and 4 * (seq // MLSTM_L) <= MLSTM_L

    h = x.reshape(t, d)

    qkv_cols = 2 * M_HEADS * M_QK_DIM + M_HEADS * M_V_DIM
    o_cols = M_HEADS * M_V_DIM
    w_in = m_w_in[0].astype(BF16)
    w_gates = jnp.pad(m_w_in[0][:, qkv_cols + o_cols:], ((0, 0), (0, LANES - 2 * M_HEADS))).astype(BF16)
    qkv, o_gate, gates = _w_in_proj(h, a_norm[0][None, :], w_in, w_gates, qkv_cols, o_cols)

    nc = seq // MLSTM_L
    gates = gates[:, :2 * M_HEADS].reshape(batch, nc, MLSTM_L, 2, M_HEADS)
    gates_r = gates.transpose(0, 4, 3, 1, 2)
    bias = jnp.stack([m_b_igate[0], m_b_fgate[0]], axis=1)
    hg, w_up_0, w_down_0 = _mlstm(qkv, o_gate, gates_r, bias[:, :, None], m_w_hnorm[0][:, None, :],
                                  (w_up, w_down), 0, batch, seq)
    h = _matmul_residual(hg, m_w_out[0].astype(BF16), h, RES_TM, RES_TN, "w_out_residual")

    f_gain = f_norm[:, None, :]
    conv_b3 = conv_b[:, None, :]
    h = _conv_ffn(h, 0, f_gain, w_up_0, conv_w, conv_b3, w_down_0, final_norm[None, :], seq, False, "conv_ffn_0")

    layer = 1
    lambda_init = 0.8 - 0.6 * math.exp(-0.3 * layer)
    half = ROPE_DIM // 2
    inv_freq_col = (ROPE_THETA ** (-jnp.arange(half, dtype=F32) / half))[:, None]
    k_cols = A_HEADS * 2 * A_QK_DIM
    pos_rows = positions.reshape(t // PROJ_TM, 1, PROJ_TM)
    kvq = _kvq_proj(h, pos_rows, inv_freq_col, kv_norm[None, :], b_norm[0][None, :],
                    w_kv.astype(BF16), w_q[0].astype(BF16), k_cols)
    lam_vecs = jnp.stack([lam_q1[0], lam_k1[0], lam_q2[0], lam_k2[0]], axis=0)
    att, w_up_1, w_down_1 = _diff_attention(kvq, lam_vecs, subln[0][:, None], (w_up, w_down), 1,
                                            batch, seq, lambda_init)
    h = _matmul_residual(att, w_o[0].astype(BF16), h, RES_TM, RES_TN, "w_o_residual")

    h = _conv_ffn(h, 1, f_gain, w_up_1, conv_w, conv_b3, w_down_1, final_norm[None, :], seq, True, "conv_ffn_1")
    return h.reshape(batch, seq, d)
```

```python
import functools
import math

import jax
import jax.numpy as jnp
from jax import lax
from jax.experimental import pallas as pl
from jax.experimental.pallas import tpu as pltpu

F32 = jnp.float32
BF16 = jnp.bfloat16

EPS = 1e-6
M_HEADS = 8
M_QK_DIM = 128
M_V_DIM = 256
GATE_SOFTCAP = 15.0
A_HEADS = 8
A_QK_DIM = 128
A_V_DIM = 256
ROPE_DIM = 32
ROPE_THETA = 500000.0
CONV_W = 3

LANES = 128
V7X_VMEM_LIMIT_BYTES = 56 * 1024 * 1024

PROJ_TM = 1024
W_IN_TN = 1024
KVQ_TN = 1024
KVQ_ROWS = 512
RES_TM = 512
RES_TN = 512
FFN_TM = 1024
FFN_ROWS = 512
FFN_TF = 512
FFN_TAIL = 8
ATT_T = 256
MLSTM_L = LANES
MLSTM_HEADS = 2

NT_DIMS = (((1,), (1,)), ((), ()))
Q_SCALE_LOG2 = (A_QK_DIM ** -0.5) * math.log2(math.e)


def _rms_scale(x):
    return x * lax.rsqrt(jnp.mean(x * x, axis=-1, keepdims=True) + EPS)


def _params(semantics):
    return pltpu.CompilerParams(dimension_semantics=semantics, vmem_limit_bytes=V7X_VMEM_LIMIT_BYTES)


def _cast_specs(weights, steps, step_of):
    in_specs, out_specs, out_shapes = [], [], []
    for w, layer in weights:
        rows, cols = w.shape[1:]
        assert rows % steps == 0
        blk = rows // steps
        in_specs.append(pl.BlockSpec((None, blk, cols), lambda *idx, layer=layer: (layer, step_of(*idx), 0)))
        out_specs.append(pl.BlockSpec((blk, cols), lambda *idx: (step_of(*idx), 0)))
        out_shapes.append(jax.ShapeDtypeStruct((rows, cols), BF16))
    return in_specs, out_specs, out_shapes


def _cast_blocks(src_refs, dst_refs):
    for src, dst in zip(src_refs, dst_refs):
        dst[...] = src[...].astype(BF16)


def _w_in_kernel(x_ref, g_ref, w_ref, wg_ref, qkv_ref, o_ref, gates_ref, xn_ref, *, qkv_tiles):
    j = pl.program_id(1)

    @pl.when(j == 0)
    def _():
        xn = (_rms_scale(x_ref[...]) * g_ref[...]).astype(BF16)
        xn_ref[...] = xn
        gates_ref[...] = jnp.dot(xn, wg_ref[...], preferred_element_type=F32)
        y = jnp.dot(xn, w_ref[...], preferred_element_type=F32)
        qkv_ref[...] = (y * (M_QK_DIM ** -0.5)).astype(BF16)

    @pl.when((j > 0) & (j < qkv_tiles))
    def _():
        qkv_ref[...] = jnp.dot(xn_ref[...], w_ref[...], preferred_element_type=F32).astype(BF16)

    @pl.when(j >= qkv_tiles)
    def _():
        o_ref[...] = jnp.dot(xn_ref[...], w_ref[...], preferred_element_type=F32)


def _w_in_proj(x, g, w, w_gates, qkv_cols, o_cols):
    t, d = x.shape
    tm, tn = PROJ_TM, W_IN_TN
    assert M_HEADS * M_QK_DIM == tn and qkv_cols % tn == 0 and o_cols % tn == 0
    qkv_tiles = qkv_cols // tn
    return pl.pallas_call(
        functools.partial(_w_in_kernel, qkv_tiles=qkv_tiles),
        grid=(t // tm, (qkv_cols + o_cols) // tn),
        in_specs=[
            pl.BlockSpec((tm, d), lambda i, j: (i, 0)),
            pl.BlockSpec((1, d), lambda i, j: (0, 0)),
            pl.BlockSpec((d, tn), lambda i, j: (0, j)),
            pl.BlockSpec((d, LANES), lambda i, j: (0, 0)),
        ],
        out_specs=[
            pl.BlockSpec((tm, tn), lambda i, j: (i, jnp.minimum(j, qkv_tiles - 1))),
            pl.BlockSpec((tm, tn), lambda i, j: (i, jnp.maximum(j - qkv_tiles, 0))),
            pl.BlockSpec((tm, LANES), lambda i, j: (i, 0)),
        ],
        out_shape=[
            jax.ShapeDtypeStruct((t, qkv_cols), BF16),
            jax.ShapeDtypeStruct((t, o_cols), F32),
            jax.ShapeDtypeStruct((t, LANES), F32),
        ],
        scratch_shapes=[pltpu.VMEM((tm, d), BF16)],
        compiler_params=_params(("parallel", "arbitrary")),
        name="norm_w_in",
    )(x, g, w, w_gates)


def _softcap(t):
    return GATE_SOFTCAP * jnp.tanh(t / GATE_SOFTCAP)


def _log_sigmoid(x):
    return jnp.minimum(x, 0.0) - jnp.log1p(jnp.exp(-jnp.abs(x)))


def _lane_scan(x, op, fill):
    lane = lax.broadcasted_iota(jnp.int32, x.shape, 1)
    d = 1
    while d < x.shape[1]:
        x = op(x, jnp.where(lane >= d, pltpu.roll(x, d, 1), fill))
        d *= 2
    return x


def _mlstm_gate_tables(gr, bias, nc):
    L = MLSTM_L
    i_rows = _softcap(gr[0] + bias[0:1, :])
    f_rows = _log_sigmoid(_softcap(gr[1] + bias[1:2, :]))
    b_rows = _lane_scan(f_rows, jnp.add, 0.0)
    a_rows = i_rows - b_rows
    pm_rows = _lane_scan(a_rows, jnp.maximum, -jnp.inf)
    b_last = b_rows[:, L - 1:L]
    a_max = pm_rows[:, L - 1:L]

    m = jnp.zeros((1, 1), F32)
    ms = [m]
    for c in range(nc):
        m = b_last[c:c + 1, :] + jnp.maximum(m, a_max[c:c + 1, :])
        ms.append(m)
    m_in = jnp.concatenate(ms[:nc], axis=0)
    m_out = jnp.concatenate(ms[1:], axis=0)
    big_m_rows = jnp.maximum(m_in, pm_rows)
    decay = jnp.exp(b_last + m_in - m_out)

    stacked = jnp.concatenate([
        big_m_rows,
        jnp.exp(m_in - big_m_rows),
        jnp.exp(-(b_rows + big_m_rows)),
        jnp.exp(a_rows + b_last - m_out),
        jnp.zeros((L - 4 * nc, L), F32)], axis=0)
    return a_rows, stacked.T, decay


def _mlstm_kernel(q_ref, k_ref, v_ref, o_ref, gr_ref, br_ref, wn_ref, *rest, seq, n_cast):
    cast_src, out_ref, cast_dst = rest[:n_cast], rest[n_cast], rest[n_cast + 1:]
    L = MLSTM_L
    nc = seq // L
    dk, dv = M_QK_DIM, M_V_DIM
    heads = range(MLSTM_HEADS)
    _cast_blocks(cast_src, cast_dst)
    tables = [_mlstm_gate_tables(gr_ref[hh], br_ref[hh], nc) for hh in heads]

    row_t = lax.broadcasted_iota(jnp.int32, (L, L), 0)
    col_s = lax.broadcasted_iota(jnp.int32, (L, L), 1)
    causal = row_t >= col_s
    ones_blk = jnp.ones((L, LANES), BF16)

    states = [jnp.zeros((dk, dv + LANES), F32) for _ in heads]
    for c in range(nc):
        rows = slice(c * L, (c + 1) * L)
        for hh in heads:
            a_rows, cols, decay = tables[hh]
            qb = q_ref[rows, hh * dk:(hh + 1) * dk]
            kb = k_ref[rows, hh * dk:(hh + 1) * dk]
            v_aug = jnp.concatenate([v_ref[rows, hh * dv:(hh + 1) * dv], ones_blk], axis=1)
            big_m = cols[:, c:c + 1]
            w_inter = cols[:, nc + c:nc + c + 1]
            e_neg_m = cols[:, 2 * nc + c:2 * nc + c + 1]
            w_key = cols[:, 3 * nc + c:3 * nc + c + 1]

            e = jnp.exp(jnp.where(causal, a_rows[c:c + 1, :] - big_m, -jnp.inf))
            s = lax.dot_general(qb, kb, NT_DIMS, preferred_element_type=F32) * e
            nd = (w_inter * jnp.dot(qb, states[hh].astype(BF16), preferred_element_type=F32)
                  + jnp.dot(s.astype(BF16), v_aug, preferred_element_type=F32))
            den = nd[:, dv:]
            r = 1.0 / jnp.maximum(jnp.abs(den), e_neg_m)
            h = nd[:, :dv] * jnp.concatenate([r] * (dv // LANES), axis=1)

            hn = _rms_scale(h) * wn_ref[hh]
            gate = jax.nn.sigmoid(o_ref[rows, hh * dv:(hh + 1) * dv])
            out_ref[rows, hh * dv:(hh + 1) * dv] = (hn * gate).astype(out_ref.dtype)

            kw_t = (kb.astype(F32) * w_key).T.astype(BF16)
            states[hh] = decay[c:c + 1, :] * states[hh] + jnp.dot(kw_t, v_aug, preferred_element_type=F32)


def _mlstm(qkv, o_gate, gates_r, bias_r, w_hnorm, cast_weights, batch, seq):
    nc = seq // MLSTM_L
    g = MLSTM_HEADS
    groups = M_HEADS // g
    cast_in, cast_out, cast_shapes = _cast_specs(cast_weights, batch * groups, lambda b, h: b * groups + h)
    qk_w, v_w = g * M_QK_DIM, g * M_V_DIM
    k_off = M_HEADS * M_QK_DIM // qk_w
    v_off = 2 * M_HEADS * M_QK_DIM // v_w
    return pl.pallas_call(
        functools.partial(_mlstm_kernel, seq=seq, n_cast=len(cast_weights)),
        grid=(batch, groups),
        in_specs=[
            pl.BlockSpec((seq, qk_w), lambda b, h: (b, h)),
            pl.BlockSpec((seq, qk_w), lambda b, h: (b, k_off + h)),
            pl.BlockSpec((seq, v_w), lambda b, h: (b, v_off + h)),
            pl.BlockSpec((seq, v_w), lambda b, h: (b, h)),
            pl.BlockSpec((None, g, 2, nc, MLSTM_L), lambda b, h: (b, h, 0, 0, 0)),
            pl.BlockSpec((g, 2, 1), lambda b, h: (h, 0, 0)),
            pl.BlockSpec((g, 1, M_V_DIM), lambda b, h: (h, 0, 0)),
        ] + cast_in,
        out_specs=[pl.BlockSpec((seq, v_w), lambda b, h: (b, h))] + cast_out,
        out_shape=[jax.ShapeDtypeStruct((batch * seq, M_HEADS * M_V_DIM), BF16)] + cast_shapes,
        compiler_params=_params(("parallel", "parallel")),
        name="mlstm_chunkwise",
    )(qkv, qkv, qkv, o_gate, gates_r, bias_r, w_hnorm, *[w for w, _ in cast_weights])


def _matmul_residual_kernel(a_ref, w_ref, r_ref, o_ref, *, tn):
    a = a_ref[...]
    for j in range(w_ref.shape[1] // tn):
        cols = slice(j * tn, (j + 1) * tn)
        o_ref[:, cols] = r_ref[:, cols] + jnp.dot(a, w_ref[:, cols], preferred_element_type=F32)


def _matmul_residual(a, w, res, tm, tn, name):
    t, kdim = a.shape
    n = w.shape[1]
    return pl.pallas_call(
        functools.partial(_matmul_residual_kernel, tn=tn),
        grid=(t // tm,),
        in_specs=[
            pl.BlockSpec((tm, kdim), lambda i: (i, 0)),
            pl.BlockSpec((kdim, n), lambda i: (0, 0)),
            pl.BlockSpec((tm, n), lambda i: (i, 0)),
        ],
        out_specs=pl.BlockSpec((tm, n), lambda i: (i, 0)),
        out_shape=jax.ShapeDtypeStruct((t, n), F32),
        compiler_params=_params(("parallel",)),
        name=name,
    )(a, w, res)


def _conv_ffn_kernel(h_ref, g_ref, wg_ref, wv_ref, cwg_ref, cwv_ref, cbg_ref, cbv_ref, wd_ref,
                     gf_ref, o_ref, xn_ref, tail_ref, *, tiles_per_seq, final_norm):
    i = pl.program_id(0)
    f = pl.program_id(1)
    tm = h_ref.shape[0]

    @pl.when(f == 0)
    def _():
        x = h_ref[...]
        xn_ref[...] = (_rms_scale(x) * g_ref[...]).astype(BF16)
        o_ref[...] = x

    @pl.when(i % tiles_per_seq == 0)
    def _():
        tail_ref[f] = jnp.zeros(tail_ref.shape[1:], F32)

    history = [tail_ref[f, 0], tail_ref[f, 1]]
    for r in range(tm // FFN_ROWS):
        rows = slice(r * FFN_ROWS, (r + 1) * FFN_ROWS)
        xn = xn_ref[rows, :]

        def conv(w_ref, cw_ref, cb_ref, slot):
            u = jnp.dot(xn, w_ref[...], preferred_element_type=F32)
            ue = jnp.concatenate([history[slot], u], axis=0)
            history[slot] = u[FFN_ROWS - FFN_TAIL:, :]
            cw = cw_ref[...]
            c = cb_ref[...] + pltpu.roll(ue, 2, 0) * cw[0:1, :] + pltpu.roll(ue, 1, 0) * cw[1:2, :] + ue * cw[2:3, :]
            return c[FFN_TAIL:, :]

        gate = conv(wg_ref, cwg_ref, cbg_ref, 0)
        val = conv(wv_ref, cwv_ref, cbv_ref, 1)
        act = (gate * jax.nn.sigmoid(gate) * val).astype(BF16)
        o_ref[rows, :] += jnp.dot(act, wd_ref[...], preferred_element_type=F32)
    tail_ref[f, 0] = history[0]
    tail_ref[f, 1] = history[1]

    if final_norm:
        @pl.when(f == pl.num_programs(1) - 1)
        def _():
            o_ref[...] = _rms_scale(o_ref[...]) * gf_ref[...]


def _conv_ffn(h, layer, g, w_up, conv_w, conv_b, w_down, g_final, seq, final_norm, name):
    t, d = h.shape
    d_ff = w_down.shape[0]
    tm, tf = FFN_TM, FFN_TF
    nf = d_ff // tf
    kern = functools.partial(_conv_ffn_kernel, tiles_per_seq=seq // tm, final_norm=final_norm)
    return pl.pallas_call(
        kern,
        grid=(t // tm, nf),
        in_specs=[
            pl.BlockSpec((tm, d), lambda i, f: (i, 0)),
            pl.BlockSpec((None, 1, d), lambda i, f: (layer, 0, 0)),
            pl.BlockSpec((d, tf), lambda i, f: (0, f)),
            pl.BlockSpec((d, tf), lambda i, f: (0, nf + f)),
            pl.BlockSpec((None, CONV_W, tf), lambda i, f: (layer, 0, f)),
            pl.BlockSpec((None, CONV_W, tf), lambda i, f: (layer, 0, nf + f)),
            pl.BlockSpec((None, 1, tf), lambda i, f: (layer, 0, f)),
            pl.BlockSpec((None, 1, tf), lambda i, f: (layer, 0, nf + f)),
            pl.BlockSpec((tf, d), lambda i, f: (f, 0)),
            pl.BlockSpec((1, d), lambda i, f: (0, 0)),
        ],
        out_specs=pl.BlockSpec((tm, d), lambda i, f: (i, 0)),
        out_shape=jax.ShapeDtypeStruct((t, d), F32),
        scratch_shapes=[
            pltpu.VMEM((tm, d), BF16),
            pltpu.VMEM((nf, 2, FFN_TAIL, tf), F32),
        ],
        compiler_params=_params(("arbitrary", "arbitrary")),
        name=name,
    )(h, g, w_up, w_up, conv_w, conv_w, conv_b, conv_b, w_down, g_final)


def _rope(y, cos_t, sin_lo, sin_hi):
    outs = []
    for gidx in range(y.shape[1] // LANES):
        blk = y[:, gidx * LANES:(gidx + 1) * LANES]
        half = ROPE_DIM // 2
        outs.append(blk * cos_t + pltpu.roll(blk, half, 1) * sin_hi + pltpu.roll(blk, LANES - half, 1) * sin_lo)
    return jnp.concatenate(outs, axis=1)


def _kvq_kernel(x_ref, pos_ref, invf_ref, gkv_ref, gq_ref, wkv_ref, wq_ref, o_ref,
                xkv_ref, xq_ref, cos_ref, slo_ref, shi_ref, *, k_tiles, kv_tiles):
    j = pl.program_id(1)

    @pl.when(j == 0)
    def _():
        y = _rms_scale(x_ref[...])
        xkv_ref[...] = (y * gkv_ref[...]).astype(BF16)
        xq_ref[...] = (y * gq_ref[...]).astype(BF16)
        half = ROPE_DIM // 2
        ang = invf_ref[...] * pos_ref[...].astype(F32)
        cos = jnp.cos(ang)
        sin = jnp.sin(ang)
        pad = LANES - ROPE_DIM
        tm = ang.shape[1]
        cos_ref[...] = jnp.concatenate([cos, cos, jnp.ones((pad, tm), F32)], axis=0).T
        sin_t = jnp.concatenate([-sin, sin, jnp.zeros((pad, tm), F32)], axis=0).T
        lane = lax.broadcasted_iota(jnp.int32, sin_t.shape, 1)
        slo_ref[...] = jnp.where(lane < half, sin_t, 0.0)
        shi_ref[...] = jnp.where(lane >= half, sin_t, 0.0)

    halves = [slice(r * KVQ_ROWS, (r + 1) * KVQ_ROWS) for r in range(x_ref.shape[0] // KVQ_ROWS)]

    def rotated(xn_ref, w_ref, rows, scale):
        y = jnp.dot(xn_ref[rows, :], w_ref[...], preferred_element_type=F32)
        y = _rope(y, cos_ref[rows, :], slo_ref[rows, :], shi_ref[rows, :])
        return y if scale is None else y * scale

    @pl.when(j < k_tiles)
    def _():
        for rows in halves:
            o_ref[rows, :] = rotated(xkv_ref, wkv_ref, rows, None).astype(BF16)

    @pl.when((j >= k_tiles) & (j < kv_tiles))
    def _():
        o_ref[...] = jnp.dot(xkv_ref[...], wkv_ref[...], preferred_element_type=F32).astype(BF16)

    @pl.when(j >= kv_tiles)
    def _():
        for rows in halves:
            o_ref[rows, :] = rotated(xq_ref, wq_ref, rows, Q_SCALE_LOG2).astype(BF16)


def _kvq_proj(h, pos_rows, inv_freq_col, g_kv, g_q, w_kv, w_q, k_cols):
    t, d = h.shape
    tm, tn = PROJ_TM, KVQ_TN
    k_tiles = k_cols // tn
    kv_tiles = w_kv.shape[1] // tn
    q_tiles = w_q.shape[1] // tn
    kern = functools.partial(_kvq_kernel, k_tiles=k_tiles, kv_tiles=kv_tiles)
    return pl.pallas_call(
        kern,
        grid=(t // tm, kv_tiles + q_tiles),
        in_specs=[
            pl.BlockSpec((tm, d), lambda i, j: (i, 0)),
            pl.BlockSpec((None, 1, tm), lambda i, j: (i, 0, 0)),
            pl.BlockSpec((ROPE_DIM // 2, 1), lambda i, j: (0, 0)),
            pl.BlockSpec((1, d), lambda i, j: (0, 0)),
            pl.BlockSpec((1, d), lambda i, j: (0, 0)),
            pl.BlockSpec((d, tn), lambda i, j: (0, jnp.minimum(j, kv_tiles - 1))),
            pl.BlockSpec((d, tn), lambda i, j: (0, jnp.maximum(j - kv_tiles, 0))),
        ],
        out_specs=pl.BlockSpec((tm, tn), lambda i, j: (i, j)),
        out_shape=jax.ShapeDtypeStruct((t, (kv_tiles + q_tiles) * tn), BF16),
        scratch_shapes=[
            pltpu.VMEM((tm, d), BF16),
            pltpu.VMEM((tm, d), BF16),
            pltpu.VMEM((tm, LANES), F32),
            pltpu.VMEM((tm, LANES), F32),
            pltpu.VMEM((tm, LANES), F32),
        ],
        compiler_params=_params(("parallel", "arbitrary")),
        name="norm_kvq_rope",
    )(h, pos_rows, inv_freq_col, g_kv, g_q, w_kv, w_q)


def _diff_attn_kernel(k_ref, v_ref, q_ref, lam_ref, g_ref, *rest, lambda_init, seq, n_cast):
    cast_src, o_ref, cast_dst = rest[:n_cast], rest[n_cast], rest[n_cast + 1:]
    tb = ATT_T
    _cast_blocks(cast_src, cast_dst)
    dk = A_QK_DIM
    nblk = seq // tb

    lv = lam_ref[...]
    lam = (jnp.exp(jnp.sum(lv[0:1, :] * lv[1:2, :], axis=1, keepdims=True))
           - jnp.exp(jnp.sum(lv[2:3, :] * lv[3:4, :], axis=1, keepdims=True)) + lambda_init)
    g_col = g_ref[...] * (1.0 - lambda_init)
    key_idx = lax.broadcasted_iota(jnp.int32, (tb, tb), 0)
    qry_idx = lax.broadcasted_iota(jnp.int32, (tb, tb), 1)
    diag_mask = key_idx <= qry_idx
    v_t = [v_ref[kb * tb:(kb + 1) * tb, :].astype(F32).T.astype(BF16) for kb in range(nblk)]

    for qi in range(nblk):
        q = q_ref[qi * tb:(qi + 1) * tb, :]
        comps = []
        for c in range(2):
            qc = q[:, c * dk:(c + 1) * dk]
            m = l = acc = None
            for kb in range(qi + 1):
                kc = k_ref[kb * tb:(kb + 1) * tb, c * dk:(c + 1) * dk]
                s = lax.dot_general(kc, qc, NT_DIMS, preferred_element_type=F32)
                if kb == qi:
                    s = jnp.where(diag_mask, s, -jnp.inf)
                s_max = jnp.max(s, axis=0, keepdims=True)
                if kb == 0:
                    m = s_max
                    p = jnp.exp2(s - m)
                    l = jnp.sum(p, axis=0, keepdims=True)
                    acc = jnp.dot(v_t[kb], p.astype(BF16), preferred_element_type=F32)
                else:
                    m_new = jnp.maximum(m, s_max)
                    alpha = jnp.exp2(m - m_new)
                    p = jnp.exp2(s - m_new)
                    l = alpha * l + jnp.sum(p, axis=0, keepdims=True)
                    acc = alpha * acc + jnp.dot(v_t[kb], p.astype(BF16), preferred_element_type=F32)
                    m = m_new
            comps.append(acc * (1.0 / l))
        o_t = comps[0] - lam * comps[1]
        o_t = o_t * lax.rsqrt(jnp.mean(o_t * o_t, axis=0, keepdims=True) + EPS) * g_col
        o_ref[qi * tb:(qi + 1) * tb, :] = o_t.T.astype(o_ref.dtype)


def _diff_attention(kvq, lam_vecs, g_subln_col, cast_weights, batch, seq, lambda_init):
    hv = A_V_DIM
    cast_in, cast_out, cast_shapes = _cast_specs(cast_weights, batch * A_HEADS, lambda b, h: b * A_HEADS + h)
    return pl.pallas_call(
        functools.partial(_diff_attn_kernel, lambda_init=lambda_init, seq=seq, n_cast=len(cast_weights)),
        grid=(batch, A_HEADS),
        in_specs=[
            pl.BlockSpec((seq, hv), lambda b, h: (b, h)),
            pl.BlockSpec((seq, hv), lambda b, h: (b, A_HEADS + h)),
            pl.BlockSpec((seq, hv), lambda b, h: (b, 2 * A_HEADS + h)),
            pl.BlockSpec((4, A_QK_DIM), lambda b, h: (0, 0)),
            pl.BlockSpec((hv, 1), lambda b, h: (0, 0)),
        ] + cast_in,
        out_specs=[pl.BlockSpec((seq, hv), lambda b, h: (b, h))] + cast_out,
        out_shape=[jax.ShapeDtypeStruct((batch * seq, A_HEADS * hv), BF16)] + cast_shapes,
        compiler_params=_params(("parallel", "parallel")),
        name="diff_attention",
    )(kvq, kvq, kvq, lam_vecs, g_subln_col, *[w for w, _ in cast_weights])


def kernel(x, positions, a_norm, m_w_in, m_b_igate, m_b_fgate, m_w_hnorm, m_w_out, kv_norm, w_kv, b_norm, w_q, lam_q1, lam_k1, lam_q2, lam_k2, subln, w_o, f_norm, w_up, conv_w, conv_b, w_down, final_norm):
    batch, seq, d = x.shape
    t = batch * seq
    depth = f_norm.shape[0]
    assert depth == 2 and a_norm.shape[0] == 1 and b_norm.shape[0] == 1
    assert seq % FFN_TM == 0 and seq % ATT_T == 0 and t % PROJ_TM == 0
    assert seq % MLSTM_L == 0 and 4 * (seq // MLSTM_L) <= MLSTM_L

    h = x.reshape(t, d)

    qkv_cols = 2 * M_HEADS * M_QK_DIM + M_HEADS * M_V_DIM
    o_cols = M_HEADS * M_V_DIM
    w_in = m_w_in[0].astype(BF16)
    w_gates = jnp.pad(m_w_in[0][:, qkv_cols + o_cols:], ((0, 0), (0, LANES - 2 * M_HEADS))).astype(BF16)
    qkv, o_gate, gates = _w_in_proj(h, a_norm[0][None, :], w_in, w_gates, qkv_cols, o_cols)

    nc = seq // MLSTM_L
    gates = gates[:, :2 * M_HEADS].reshape(batch, nc, MLSTM_L, 2, M_HEADS)
    gates_r = gates.transpose(0, 4, 3, 1, 2)
    bias = jnp.stack([m_b_igate[0], m_b_fgate[0]], axis=1)
    hg, w_up_0, w_down_0, w_out_b, w_q_b = _mlstm(
        qkv, o_gate, gates_r, bias[:, :, None], m_w_hnorm[0][:, None, :],
        ((w_up, 0), (w_down, 0), (m_w_out, 0), (w_q, 0)), batch, seq)
    h = _matmul_residual(hg, w_out_b, h, RES_TM, RES_TN, "w_out_residual")

    f_gain = f_norm[:, None, :]
    conv_b3 = conv_b[:, None, :]
    h = _conv_ffn(h, 0, f_gain, w_up_0, conv_w, conv_b3, w_down_0, final_norm[None, :], seq, False, "conv_ffn_0")

    layer = 1
    lambda_init = 0.8 - 0.6 * math.exp(-0.3 * layer)
    half = ROPE_DIM // 2
    inv_freq_col = (ROPE_THETA ** (-jnp.arange(half, dtype=F32) / half))[:, None]
    k_cols = A_HEADS * 2 * A_QK_DIM
    pos_rows = positions.reshape(t // PROJ_TM, 1, PROJ_TM)
    kvq = _kvq_proj(h, pos_rows, inv_freq_col, kv_norm[None, :], b_norm[0][None, :],
                    w_kv.astype(BF16), w_q_b, k_cols)
    lam_vecs = jnp.stack([lam_q1[0], lam_k1[0], lam_q2[0], lam_k2[0]], axis=0)
    att, w_up_1, w_down_1, w_o_b = _diff_attention(kvq, lam_vecs, subln[0][:, None],
                                                   ((w_up, 1), (w_down, 1), (w_o, 0)), batch, seq, lambda_init)
    h = _matmul_residual(att, w_o_b, h, RES_TM, RES_TN, "w_o_residual")

    h = _conv_ffn(h, 1, f_gain, w_up_1, conv_w, conv_b3, w_down_1, final_norm[None, :], seq, True, "conv_ffn_1")
    return h.reshape(batch, seq, d)
```

```python
import functools
import math

import jax
import jax.numpy as jnp
from jax import lax
from jax.experimental import pallas as pl
from jax.experimental.pallas import tpu as pltpu

F32 = jnp.float32
BF16 = jnp.bfloat16

EPS = 1e-6
M_HEADS = 8
M_QK_DIM = 128
M_V_DIM = 256
GATE_SOFTCAP = 15.0
A_HEADS = 8
A_QK_DIM = 128
A_V_DIM = 256
ROPE_DIM = 32
ROPE_THETA = 500000.0
CONV_W = 3

LANES = 128
V7X_VMEM_LIMIT_BYTES = 56 * 1024 * 1024

PROJ_TM = 1024
W_IN_TN = 1024
W_IN_CAST_STEPS = 2
KVQ_TN = 1024
KVQ_ROWS = 512
RES_TM = 512
RES_TN = 512
FFN_TM = 1024
FFN_ROWS = 512
FFN_TF = 512
FFN_TAIL = 8
ATT_T = 256
MLSTM_L = LANES
MLSTM_HEADS = 2

NT_DIMS = (((1,), (1,)), ((), ()))
Q_SCALE_LOG2 = (A_QK_DIM ** -0.5) * math.log2(math.e)


def _rms_scale(x):
    return x * lax.rsqrt(jnp.mean(x * x, axis=-1, keepdims=True) + EPS)


def _params(semantics):
    return pltpu.CompilerParams(dimension_semantics=semantics, vmem_limit_bytes=V7X_VMEM_LIMIT_BYTES)


def _cast_specs(weights, steps, step_of):
    in_specs, out_specs, out_shapes = [], [], []
    for w, layer in weights:
        rows, cols = w.shape[1:]
        assert rows % steps == 0
        blk = rows // steps
        in_specs.append(pl.BlockSpec((None, blk, cols), lambda *idx, layer=layer: (layer, step_of(*idx), 0)))
        out_specs.append(pl.BlockSpec((blk, cols), lambda *idx: (step_of(*idx), 0)))
        out_shapes.append(jax.ShapeDtypeStruct((rows, cols), BF16))
    return in_specs, out_specs, out_shapes


def _cast_blocks(src_refs, dst_refs):
    for src, dst in zip(src_refs, dst_refs):
        dst[...] = src[...].astype(BF16)


def _w_in_kernel(x_ref, g_ref, w_ref, wg_ref, cast_ref, qkv_ref, o_ref, gates_ref, cast_out_ref, xn_ref,
                 *, qkv_tiles):
    j = pl.program_id(1)

    @pl.when(j < W_IN_CAST_STEPS)
    def _():
        _cast_blocks((cast_ref,), (cast_out_ref,))

    @pl.when(j == 0)
    def _():
        xn = (_rms_scale(x_ref[...]) * g_ref[...]).astype(BF16)
        xn_ref[...] = xn
        gates_ref[...] = jnp.dot(xn, wg_ref[...], preferred_element_type=F32)
        y = jnp.dot(xn, w_ref[...], preferred_element_type=F32)
        qkv_ref[...] = (y * (M_QK_DIM ** -0.5)).astype(BF16)

    @pl.when((j > 0) & (j < qkv_tiles))
    def _():
        qkv_ref[...] = jnp.dot(xn_ref[...], w_ref[...], preferred_element_type=F32).astype(BF16)

    @pl.when(j >= qkv_tiles)
    def _():
        o_ref[...] = jnp.dot(xn_ref[...], w_ref[...], preferred_element_type=F32)


def _w_in_proj(x, g, w, w_gates, cast_weight, qkv_cols, o_cols):
    t, d = x.shape
    tm, tn = PROJ_TM, W_IN_TN
    assert M_HEADS * M_QK_DIM == tn and qkv_cols % tn == 0 and o_cols % tn == 0
    qkv_tiles = qkv_cols // tn
    cast_in, cast_out, cast_shapes = _cast_specs(
        (cast_weight,), (t // tm) * W_IN_CAST_STEPS,
        lambda i, j: i * W_IN_CAST_STEPS + jnp.minimum(j, W_IN_CAST_STEPS - 1))
    return pl.pallas_call(
        functools.partial(_w_in_kernel, qkv_tiles=qkv_tiles),
        grid=(t // tm, (qkv_cols + o_cols) // tn),
        in_specs=[
            pl.BlockSpec((tm, d), lambda i, j: (i, 0)),
            pl.BlockSpec((1, d), lambda i, j: (0, 0)),
            pl.BlockSpec((d, tn), lambda i, j: (0, j)),
            pl.BlockSpec((d, LANES), lambda i, j: (0, 0)),
        ] + cast_in,
        out_specs=[
            pl.BlockSpec((tm, tn), lambda i, j: (i, jnp.minimum(j, qkv_tiles - 1))),
            pl.BlockSpec((tm, tn), lambda i, j: (i, jnp.maximum(j - qkv_tiles, 0))),
            pl.BlockSpec((tm, LANES), lambda i, j: (i, 0)),
        ] + cast_out,
        out_shape=[
            jax.ShapeDtypeStruct((t, qkv_cols), BF16),
            jax.ShapeDtypeStruct((t, o_cols), F32),
            jax.ShapeDtypeStruct((t, LANES), F32),
        ] + cast_shapes,
        scratch_shapes=[pltpu.VMEM((tm, d), BF16)],
        compiler_params=_params(("parallel", "arbitrary")),
        name="norm_w_in",
    )(x, g, w, w_gates, cast_weight[0])


def _softcap(t):
    return GATE_SOFTCAP * jnp.tanh(t / GATE_SOFTCAP)


def _log_sigmoid(x):
    return jnp.minimum(x, 0.0) - jnp.log1p(jnp.exp(-jnp.abs(x)))


def _lane_scan(x, op, fill):
    lane = lax.broadcasted_iota(jnp.int32, x.shape, 1)
    d = 1
    while d < x.shape[1]:
        x = op(x, jnp.where(lane >= d, pltpu.roll(x, d, 1), fill))
        d *= 2
    return x


def _mlstm_gate_tables(gr, bias, nc):
    L = MLSTM_L
    i_rows = _softcap(gr[0] + bias[0:1, :])
    f_rows = _log_sigmoid(_softcap(gr[1] + bias[1:2, :]))
    b_rows = _lane_scan(f_rows, jnp.add, 0.0)
    a_rows = i_rows - b_rows
    pm_rows = _lane_scan(a_rows, jnp.maximum, -jnp.inf)
    b_last = b_rows[:, L - 1:L]
    a_max = pm_rows[:, L - 1:L]

    m = jnp.zeros((1, 1), F32)
    ms = [m]
    for c in range(nc):
        m = b_last[c:c + 1, :] + jnp.maximum(m, a_max[c:c + 1, :])
        ms.append(m)
    m_in = jnp.concatenate(ms[:nc], axis=0)
    m_out = jnp.concatenate(ms[1:], axis=0)
    big_m_rows = jnp.maximum(m_in, pm_rows)
    decay = jnp.exp(b_last + m_in - m_out)

    stacked = jnp.concatenate([
        big_m_rows,
        jnp.exp(m_in - big_m_rows),
        jnp.exp(-(b_rows + big_m_rows)),
        jnp.exp(a_rows + b_last - m_out),
        jnp.zeros((L - 4 * nc, L), F32)], axis=0)
    return a_rows, stacked.T, decay


def _mlstm_kernel(q_ref, k_ref, v_ref, o_ref, gr_ref, br_ref, wn_ref, *rest, seq, n_cast):
    cast_src, out_ref, cast_dst = rest[:n_cast], rest[n_cast], rest[n_cast + 1:]
    L = MLSTM_L
    nc = seq // L
    dk, dv = M_QK_DIM, M_V_DIM
    heads = range(MLSTM_HEADS)
    _cast_blocks(cast_src, cast_dst)
    tables = [_mlstm_gate_tables(gr_ref[hh], br_ref[hh], nc) for hh in heads]

    row_t = lax.broadcasted_iota(jnp.int32, (L, L), 0)
    col_s = lax.broadcasted_iota(jnp.int32, (L, L), 1)
    causal = row_t >= col_s
    ones_blk = jnp.ones((L, LANES), BF16)

    states = [jnp.zeros((dk, dv + LANES), F32) for _ in heads]
    for c in range(nc):
        rows = slice(c * L, (c + 1) * L)
        for hh in heads:
            a_rows, cols, decay = tables[hh]
            qb = q_ref[rows, hh * dk:(hh + 1) * dk]
            kb = k_ref[rows, hh * dk:(hh + 1) * dk]
            v_aug = jnp.concatenate([v_ref[rows, hh * dv:(hh + 1) * dv], ones_blk], axis=1)
            big_m = cols[:, c:c + 1]
            w_inter = cols[:, nc + c:nc + c + 1]
            e_neg_m = cols[:, 2 * nc + c:2 * nc + c + 1]
            w_key = cols[:, 3 * nc + c:3 * nc + c + 1]

            e = jnp.exp(jnp.where(causal, a_rows[c:c + 1, :] - big_m, -jnp.inf))
            s = lax.dot_general(qb, kb, NT_DIMS, preferred_element_type=F32) * e
            nd = (w_inter * jnp.dot(qb, states[hh].astype(BF16), preferred_element_type=F32)
                  + jnp.dot(s.astype(BF16), v_aug, preferred_element_type=F32))
            den = nd[:, dv:]
            r = 1.0 / jnp.maximum(jnp.abs(den), e_neg_m)
            h = nd[:, :dv] * jnp.concatenate([r] * (dv // LANES), axis=1)

            hn = _rms_scale(h) * wn_ref[hh]
            gate = jax.nn.sigmoid(o_ref[rows, hh * dv:(hh + 1) * dv])
            out_ref[rows, hh * dv:(hh + 1) * dv] = (hn * gate).astype(out_ref.dtype)

            kw_t = (kb.astype(F32) * w_key).T.astype(BF16)
            states[hh] = decay[c:c + 1, :] * states[hh] + jnp.dot(kw_t, v_aug, preferred_element_type=F32)


def _mlstm(qkv, o_gate, gates_r, bias_r, w_hnorm, cast_weights, batch, seq):
    nc = seq // MLSTM_L
    g = MLSTM_HEADS
    groups = M_HEADS // g
    cast_in, cast_out, cast_shapes = _cast_specs(cast_weights, batch * groups, lambda b, h: b * groups + h)
    qk_w, v_w = g * M_QK_DIM, g * M_V_DIM
    k_off = M_HEADS * M_QK_DIM // qk_w
    v_off = 2 * M_HEADS * M_QK_DIM // v_w
    return pl.pallas_call(
        functools.partial(_mlstm_kernel, seq=seq, n_cast=len(cast_weights)),
        grid=(batch, groups),
        in_specs=[
            pl.BlockSpec((seq, qk_w), lambda b, h: (b, h)),
            pl.BlockSpec((seq, qk_w), lambda b, h: (b, k_off + h)),
            pl.BlockSpec((seq, v_w), lambda b, h: (b, v_off + h)),
            pl.BlockSpec((seq, v_w), lambda b, h: (b, h)),
            pl.BlockSpec((None, g, 2, nc, MLSTM_L), lambda b, h: (b, h, 0, 0, 0)),
            pl.BlockSpec((g, 2, 1), lambda b, h: (h, 0, 0)),
            pl.BlockSpec((g, 1, M_V_DIM), lambda b, h: (h, 0, 0)),
        ] + cast_in,
        out_specs=[pl.BlockSpec((seq, v_w), lambda b, h: (b, h))] + cast_out,
        out_shape=[jax.ShapeDtypeStruct((batch * seq, M_HEADS * M_V_DIM), BF16)] + cast_shapes,
        compiler_params=_params(("parallel", "parallel")),
        name="mlstm_chunkwise",
    )(qkv, qkv, qkv, o_gate, gates_r, bias_r, w_hnorm, *[w for w, _ in cast_weights])


def _matmul_residual_kernel(a_ref, w_ref, r_ref, o_ref, *, tn):
    a = a_ref[...]
    for j in range(w_ref.shape[1] // tn):
        cols = slice(j * tn, (j + 1) * tn)
        o_ref[:, cols] = r_ref[:, cols] + jnp.dot(a, w_ref[:, cols], preferred_element_type=F32)


def _matmul_residual(a, w, res, tm, tn, name):
    t, kdim = a.shape
    n = w.shape[1]
    return pl.pallas_call(
        functools.partial(_matmul_residual_kernel, tn=tn),
        grid=(t // tm,),
        in_specs=[
            pl.BlockSpec((tm, kdim), lambda i: (i, 0)),
            pl.BlockSpec((kdim, n), lambda i: (0, 0)),
            pl.BlockSpec((tm, n), lambda i: (i, 0)),
        ],
        out_specs=pl.BlockSpec((tm, n), lambda i: (i, 0)),
        out_shape=jax.ShapeDtypeStruct((t, n), F32),
        compiler_params=_params(("parallel",)),
        name=name,
    )(a, w, res)


def _conv_ffn_kernel(h_ref, g_ref, wg_ref, wv_ref, cwg_ref, cwv_ref, cbg_ref, cbv_ref, wd_ref,
                     gf_ref, o_ref, xn_ref, tail_ref, *, tiles_per_seq, final_norm):
    i = pl.program_id(0)
    f = pl.program_id(1)
    tm = h_ref.shape[0]

    @pl.when(f == 0)
    def _():
        x = h_ref[...]
        xn_ref[...] = (_rms_scale(x) * g_ref[...]).astype(BF16)
        o_ref[...] = x

    @pl.when(i % tiles_per_seq == 0)
    def _():
        tail_ref[f] = jnp.zeros(tail_ref.shape[1:], F32)

    history = [tail_ref[f, 0], tail_ref[f, 1]]
    for r in range(tm // FFN_ROWS):
        rows = slice(r * FFN_ROWS, (r + 1) * FFN_ROWS)
        xn = xn_ref[rows, :]

        def conv(w_ref, cw_ref, cb_ref, slot):
            u = jnp.dot(xn, w_ref[...], preferred_element_type=F32)
            ue = jnp.concatenate([history[slot], u], axis=0)
            history[slot] = u[FFN_ROWS - FFN_TAIL:, :]
            cw = cw_ref[...]
            c = cb_ref[...] + pltpu.roll(ue, 2, 0) * cw[0:1, :] + pltpu.roll(ue, 1, 0) * cw[1:2, :] + ue * cw[2:3, :]
            return c[FFN_TAIL:, :]

        gate = conv(wg_ref, cwg_ref, cbg_ref, 0)
        val = conv(wv_ref, cwv_ref, cbv_ref, 1)
        act = (gate * jax.nn.sigmoid(gate) * val).astype(BF16)
        o_ref[rows, :] += jnp.dot(act, wd_ref[...], preferred_element_type=F32)
    tail_ref[f, 0] = history[0]
    tail_ref[f, 1] = history[1]

    if final_norm:
        @pl.when(f == pl.num_programs(1) - 1)
        def _():
            o_ref[...] = _rms_scale(o_ref[...]) * gf_ref[...]


def _conv_ffn(h, layer, g, w_up, conv_w, conv_b, w_down, g_final, seq, final_norm, name):
    t, d = h.shape
    d_ff = w_down.shape[0]
    tm, tf = FFN_TM, FFN_TF
    nf = d_ff // tf
    kern = functools.partial(_conv_ffn_kernel, tiles_per_seq=seq // tm, final_norm=final_norm)
    return pl.pallas_call(
        kern,
        grid=(t // tm, nf),
        in_specs=[
            pl.BlockSpec((tm, d), lambda i, f: (i, 0)),
            pl.BlockSpec((None, 1, d), lambda i, f: (layer, 0, 0)),
            pl.BlockSpec((d, tf), lambda i, f: (0, f)),
            pl.BlockSpec((d, tf), lambda i, f: (0, nf + f)),
            pl.BlockSpec((None, CONV_W, tf), lambda i, f: (layer, 0, f)),
            pl.BlockSpec((None, CONV_W, tf), lambda i, f: (layer, 0, nf + f)),
            pl.BlockSpec((None, 1, tf), lambda i, f: (layer, 0, f)),
            pl.BlockSpec((None, 1, tf), lambda i, f: (layer, 0, nf + f)),
            pl.BlockSpec((tf, d), lambda i, f: (f, 0)),
            pl.BlockSpec((1, d), lambda i, f: (0, 0)),
        ],
        out_specs=pl.BlockSpec((tm, d), lambda i, f: (i, 0)),
        out_shape=jax.ShapeDtypeStruct((t, d), F32),
        scratch_shapes=[
            pltpu.VMEM((tm, d), BF16),
            pltpu.VMEM((nf, 2, FFN_TAIL, tf), F32),
        ],
        compiler_params=_params(("arbitrary", "arbitrary")),
        name=name,
    )(h, g, w_up, w_up, conv_w, conv_w, conv_b, conv_b, w_down, g_final)


def _rope(y, cos_t, sin_lo, sin_hi):
    outs = []
    for gidx in range(y.shape[1] // LANES):
        blk = y[:, gidx * LANES:(gidx + 1) * LANES]
        half = ROPE_DIM // 2
        outs.append(blk * cos_t + pltpu.roll(blk, half, 1) * sin_hi + pltpu.roll(blk, LANES - half, 1) * sin_lo)
    return jnp.concatenate(outs, axis=1)


def _kvq_kernel(x_ref, pos_ref, invf_ref, gkv_ref, gq_ref, wkv_ref, wq_ref, o_ref,
                xkv_ref, xq_ref, cos_ref, slo_ref, shi_ref, *, k_tiles, kv_tiles):
    j = pl.program_id(1)

    @pl.when(j == 0)
    def _():
        y = _rms_scale(x_ref[...])
        xkv_ref[...] = (y * gkv_ref[...]).astype(BF16)
        xq_ref[...] = (y * gq_ref[...]).astype(BF16)
        half = ROPE_DIM // 2
        ang = invf_ref[...] * pos_ref[...].astype(F32)
        cos = jnp.cos(ang)
        sin = jnp.sin(ang)
        pad = LANES - ROPE_DIM
        tm = ang.shape[1]
        cos_ref[...] = jnp.concatenate([cos, cos, jnp.ones((pad, tm), F32)], axis=0).T
        sin_t = jnp.concatenate([-sin, sin, jnp.zeros((pad, tm), F32)], axis=0).T
        lane = lax.broadcasted_iota(jnp.int32, sin_t.shape, 1)
        slo_ref[...] = jnp.where(lane < half, sin_t, 0.0)
        shi_ref[...] = jnp.where(lane >= half, sin_t, 0.0)

    halves = [slice(r * KVQ_ROWS, (r + 1) * KVQ_ROWS) for r in range(x_ref.shape[0] // KVQ_ROWS)]

    def rotated(xn_ref, w_ref, rows, scale):
        y = jnp.dot(xn_ref[rows, :], w_ref[...], preferred_element_type=F32)
        y = _rope(y, cos_ref[rows, :], slo_ref[rows, :], shi_ref[rows, :])
        return y if scale is None else y * scale

    @pl.when(j < k_tiles)
    def _():
        for rows in halves:
            o_ref[rows, :] = rotated(xkv_ref, wkv_ref, rows, None).astype(BF16)

    @pl.when((j >= k_tiles) & (j < kv_tiles))
    def _():
        o_ref[...] = jnp.dot(xkv_ref[...], wkv_ref[...], preferred_element_type=F32).astype(BF16)

    @pl.when(j >= kv_tiles)
    def _():
        for rows in halves:
            o_ref[rows, :] = rotated(xq_ref, wq_ref, rows, Q_SCALE_LOG2).astype(BF16)


def _kvq_proj(h, pos_rows, inv_freq_col, g_kv, g_q, w_kv, w_q, k_cols):
    t, d = h.shape
    tm, tn = PROJ_TM, KVQ_TN
    k_tiles = k_cols // tn
    kv_tiles = w_kv.shape[1] // tn
    q_tiles = w_q.shape[1] // tn
    kern = functools.partial(_kvq_kernel, k_tiles=k_tiles, kv_tiles=kv_tiles)
    return pl.pallas_call(
        kern,
        grid=(t // tm, kv_tiles + q_tiles),
        in_specs=[
            pl.BlockSpec((tm, d), lambda i, j: (i, 0)),
            pl.BlockSpec((None, 1, tm), lambda i, j: (i, 0, 0)),
            pl.BlockSpec((ROPE_DIM // 2, 1), lambda i, j: (0, 0)),
            pl.BlockSpec((1, d), lambda i, j: (0, 0)),
            pl.BlockSpec((1, d), lambda i, j: (0, 0)),
            pl.BlockSpec((d, tn), lambda i, j: (0, jnp.minimum(j, kv_tiles - 1))),
            pl.BlockSpec((d, tn), lambda i, j: (0, jnp.maximum(j - kv_tiles, 0))),
        ],
        out_specs=pl.BlockSpec((tm, tn), lambda i, j: (i, j)),
        out_shape=jax.ShapeDtypeStruct((t, (kv_tiles + q_tiles) * tn), BF16),
        scratch_shapes=[
            pltpu.VMEM((tm, d), BF16),
            pltpu.VMEM((tm, d), BF16),
            pltpu.VMEM((tm, LANES), F32),
            pltpu.VMEM((tm, LANES), F32),
            pltpu.VMEM((tm, LANES), F32),
        ],
        compiler_params=_params(("parallel", "arbitrary")),
        name="norm_kvq_rope",
    )(h, pos_rows, inv_freq_col, g_kv, g_q, w_kv, w_q)


def _diff_attn_kernel(k_ref, v_ref, q_ref, lam_ref, g_ref, *rest, lambda_init, seq, n_cast):
    cast_src, o_ref, cast_dst = rest[:n_cast], rest[n_cast], rest[n_cast + 1:]
    tb = ATT_T
    _cast_blocks(cast_src, cast_dst)
    dk = A_QK_DIM
    nblk = seq // tb

    lv = lam_ref[...]
    lam = (jnp.exp(jnp.sum(lv[0:1, :] * lv[1:2, :], axis=1, keepdims=True))
           - jnp.exp(jnp.sum(lv[2:3, :] * lv[3:4, :], axis=1, keepdims=True)) + lambda_init)
    g_col = g_ref[...] * (1.0 - lambda_init)
    key_idx = lax.broadcasted_iota(jnp.int32, (tb, tb), 0)
    qry_idx = lax.broadcasted_iota(jnp.int32, (tb, tb), 1)
    diag_mask = key_idx <= qry_idx
    v_t = [v_ref[kb * tb:(kb + 1) * tb, :].astype(F32).T.astype(BF16) for kb in range(nblk)]

    for qi in range(nblk):
        q = q_ref[qi * tb:(qi + 1) * tb, :]
        comps = []
        for c in range(2):
            qc = q[:, c * dk:(c + 1) * dk]
            m = l = acc = None
            for kb in range(qi + 1):
                kc = k_ref[kb * tb:(kb + 1) * tb, c * dk:(c + 1) * dk]
                s = lax.dot_general(kc, qc, NT_DIMS, preferred_element_type=F32)
                if kb == qi:
                    s = jnp.where(diag_mask, s, -jnp.inf)
                s_max = jnp.max(s, axis=0, keepdims=True)
                if kb == 0:
                    m = s_max
                    p = jnp.exp2(s - m)
                    l = jnp.sum(p, axis=0, keepdims=True)
                    acc = jnp.dot(v_t[kb], p.astype(BF16), preferred_element_type=F32)
                else:
                    m_new = jnp.maximum(m, s_max)
                    alpha = jnp.exp2(m - m_new)
                    p = jnp.exp2(s - m_new)
                    l = alpha * l + jnp.sum(p, axis=0, keepdims=True)
                    acc = alpha * acc + jnp.dot(v_t[kb], p.astype(BF16), preferred_element_type=F32)
                    m = m_new
            comps.append(acc * (1.0 / l))
        o_t = comps[0] - lam * comps[1]
        o_t = o_t * lax.rsqrt(jnp.mean(o_t * o_t, axis=0, keepdims=True) + EPS) * g_col
        o_ref[qi * tb:(qi + 1) * tb, :] = o_t.T.astype(o_ref.dtype)


def _diff_attention(kvq, lam_vecs, g_subln_col, cast_weights, batch, seq, lambda_init):
    hv = A_V_DIM
    cast_in, cast_out, cast_shapes = _cast_specs(cast_weights, batch * A_HEADS, lambda b, h: b * A_HEADS + h)
    return pl.pallas_call(
        functools.partial(_diff_attn_kernel, lambda_init=lambda_init, seq=seq, n_cast=len(cast_weights)),
        grid=(batch, A_HEADS),
        in_specs=[
            pl.BlockSpec((seq, hv), lambda b, h: (b, h)),
            pl.BlockSpec((seq, hv), lambda b, h: (b, A_HEADS + h)),
            pl.BlockSpec((seq, hv), lambda b, h: (b, 2 * A_HEADS + h)),
            pl.BlockSpec((4, A_QK_DIM), lambda b, h: (0, 0)),
            pl.BlockSpec((hv, 1), lambda b, h: (0, 0)),
        ] + cast_in,
        out_specs=[pl.BlockSpec((seq, hv), lambda b, h: (b, h))] + cast_out,
        out_shape=[jax.ShapeDtypeStruct((batch * seq, A_HEADS * hv), BF16)] + cast_shapes,
        compiler_params=_params(("parallel", "parallel")),
        name="diff_attention",
    )(kvq, kvq, kvq, lam_vecs, g_subln_col, *[w for w, _ in cast_weights])


def kernel(x, positions, a_norm, m_w_in, m_b_igate, m_b_fgate, m_w_hnorm, m_w_out, kv_norm, w_kv, b_norm, w_q, lam_q1, lam_k1, lam_q2, lam_k2, subln, w_o, f_norm, w_up, conv_w, conv_b, w_down, final_norm):
    batch, seq, d = x.shape
    t = batch * seq
    depth = f_norm.shape[0]
    assert depth == 2 and a_norm.shape[0] == 1 and b_norm.shape[0] == 1
    assert seq % FFN_TM == 0 and seq % ATT_T == 0 and t % PROJ_TM == 0
    assert seq % MLSTM_L == 0 and 4 * (seq // MLSTM_L) <= MLSTM_L

    h = x.reshape(t, d)

    qkv_cols = 2 * M_HEADS * M_QK_DIM + M_HEADS * M_V_DIM
    o_cols = M_HEADS * M_V_DIM
    w_in = m_w_in[0][:, :qkv_cols + o_cols].astype(BF16)
    w_gates = jnp.pad(m_w_in[0][:, qkv_cols + o_cols:], ((0, 0), (0, LANES - 2 * M_HEADS))).astype(BF16)
    qkv, o_gate, gates, w_kv_b = _w_in_proj(h, a_norm[0][None, :], w_in, w_gates, (w_kv[None], 0),
                                            qkv_cols, o_cols)

    nc = seq // MLSTM_L
    gates = gates[:, :2 * M_HEADS].reshape(batch, nc, MLSTM_L, 2, M_HEADS)
    gates_r = gates.transpose(0, 4, 3, 1, 2)
    bias = jnp.stack([m_b_igate[0], m_b_fgate[0]], axis=1)
    hg, w_up_0, w_down_0, w_out_b, w_q_b = _mlstm(
        qkv, o_gate, gates_r, bias[:, :, None], m_w_hnorm[0][:, None, :],
        ((w_up, 0), (w_down, 0), (m_w_out, 0), (w_q, 0)), batch, seq)
    h = _matmul_residual(hg, w_out_b, h, RES_TM, RES_TN, "w_out_residual")

    f_gain = f_norm[:, None, :]
    conv_b3 = conv_b[:, None, :]
    h = _conv_ffn(h, 0, f_gain, w_up_0, conv_w, conv_b3, w_down_0, final_norm[None, :], seq, False, "conv_ffn_0")

    layer = 1
    lambda_init = 0.8 - 0.6 * math.exp(-0.3 * layer)
    half = ROPE_DIM // 2
    inv_freq_col = (ROPE_THETA ** (-jnp.arange(half, dtype=F32) / half))[:, None]
    k_cols = A_HEADS * 2 * A_QK_DIM
    pos_rows = positions.reshape(t // PROJ_TM, 1, PROJ_TM)
    kvq = _kvq_proj(h, pos_rows, inv_freq_col, kv_norm[None, :], b_norm[0][None, :],
                    w_kv_b, w_q_b, k_cols)
    lam_vecs = jnp.stack([lam_q1[0], lam_k1[0], lam_q2[0], lam_k2[0]], axis=0)
    att, w_up_1, w_down_1, w_o_b = _diff_attention(kvq, lam_vecs, subln[0][:, None],
                                                   ((w_up, 1), (w_down, 1), (w_o, 0)), batch, seq, lambda_init)
    h = _matmul_residual(att, w_o_b, h, RES_TM, RES_TN, "w_o_residual")

    h = _conv_ffn(h, 1, f_gain, w_up_1, conv_w, conv_b3, w_down_1, final_norm[None, :], seq, True, "conv_ffn_1")
    return h.reshape(batch, seq, d)
```

```python
import functools
import math

import jax
import jax.numpy as jnp
from jax import lax
from jax.experimental import pallas as pl
from jax.experimental.pallas import tpu as pltpu

F32 = jnp.float32
BF16 = jnp.bfloat16

EPS = 1e-6
M_HEADS = 8
M_QK_DIM = 128
M_V_DIM = 256
GATE_SOFTCAP = 15.0
A_HEADS = 8
A_QK_DIM = 128
A_V_DIM = 256
ROPE_DIM = 32
ROPE_THETA = 500000.0
CONV_W = 3

LANES = 128
V7X_VMEM_LIMIT_BYTES = 56 * 1024 * 1024

PROJ_TM = 1024
W_IN_TN = 1024
KVQ_TN = 1024
KVQ_ROWS = 512
RES_TM = 512
RES_TN = 512
FFN_TM = 1024
FFN_ROWS = 512
FFN_TF = 512
FFN_TAIL = 8
ATT_T = 256
MLSTM_L = LANES
MLSTM_HEADS = 2

NT_DIMS = (((1,), (1,)), ((), ()))
Q_SCALE_LOG2 = (A_QK_DIM ** -0.5) * math.log2(math.e)


def _rms_scale(x):
    return x * lax.rsqrt(jnp.mean(x * x, axis=-1, keepdims=True) + EPS)


def _params(semantics):
    return pltpu.CompilerParams(dimension_semantics=semantics, vmem_limit_bytes=V7X_VMEM_LIMIT_BYTES)


def _cast_specs(weights, steps, step_of):
    in_specs, out_specs, out_shapes = [], [], []
    for w, layer in weights:
        rows, cols = w.shape[1:]
        assert rows % steps == 0
        blk = rows // steps
        in_specs.append(pl.BlockSpec((None, blk, cols), lambda *idx, layer=layer: (layer, step_of(*idx), 0)))
        out_specs.append(pl.BlockSpec((blk, cols), lambda *idx: (step_of(*idx), 0)))
        out_shapes.append(jax.ShapeDtypeStruct((rows, cols), BF16))
    return in_specs, out_specs, out_shapes


def _cast_blocks(src_refs, dst_refs):
    for src, dst in zip(src_refs, dst_refs):
        dst[...] = src[...].astype(BF16)


def _w_in_kernel(x_ref, g_ref, w_ref, wg_ref, qkv_ref, o_ref, gates_ref, xn_ref, *, qkv_tiles):
    j = pl.program_id(1)

    @pl.when(j == 0)
    def _():
        xn = (_rms_scale(x_ref[...]) * g_ref[...]).astype(BF16)
        xn_ref[...] = xn
        gates_ref[...] = jnp.dot(xn, wg_ref[...], preferred_element_type=F32)
        y = jnp.dot(xn, w_ref[...], preferred_element_type=F32)
        qkv_ref[...] = (y * (M_QK_DIM ** -0.5)).astype(BF16)

    @pl.when((j > 0) & (j < qkv_tiles))
    def _():
        qkv_ref[...] = jnp.dot(xn_ref[...], w_ref[...], preferred_element_type=F32).astype(BF16)

    @pl.when(j >= qkv_tiles)
    def _():
        o_ref[...] = jnp.dot(xn_ref[...], w_ref[...], preferred_element_type=F32)


def _w_in_proj(x, g, w, w_gates, qkv_cols, o_cols):
    t, d = x.shape
    tm, tn = PROJ_TM, W_IN_TN
    assert M_HEADS * M_QK_DIM == tn and qkv_cols % tn == 0 and o_cols % tn == 0
    qkv_tiles = qkv_cols // tn
    return pl.pallas_call(
        functools.partial(_w_in_kernel, qkv_tiles=qkv_tiles),
        grid=(t // tm, (qkv_cols + o_cols) // tn),
        in_specs=[
            pl.BlockSpec((tm, d), lambda i, j: (i, 0)),
            pl.BlockSpec((1, d), lambda i, j: (0, 0)),
            pl.BlockSpec((d, tn), lambda i, j: (0, j)),
            pl.BlockSpec((d, LANES), lambda i, j: (0, 0)),
        ],
        out_specs=[
            pl.BlockSpec((tm, tn), lambda i, j: (i, jnp.minimum(j, qkv_tiles - 1))),
            pl.BlockSpec((tm, tn), lambda i, j: (i, jnp.maximum(j - qkv_tiles, 0))),
            pl.BlockSpec((tm, LANES), lambda i, j: (i, 0)),
        ],
        out_shape=[
            jax.ShapeDtypeStruct((t, qkv_cols), BF16),
            jax.ShapeDtypeStruct((t, o_cols), F32),
            jax.ShapeDtypeStruct((t, LANES), F32),
        ],
        scratch_shapes=[pltpu.VMEM((tm, d), BF16)],
        compiler_params=_params(("parallel", "arbitrary")),
        name="norm_w_in",
    )(x, g, w, w_gates)


def _softcap(t):
    return GATE_SOFTCAP * jnp.tanh(t / GATE_SOFTCAP)


def _log_sigmoid(x):
    return jnp.minimum(x, 0.0) - jnp.log1p(jnp.exp(-jnp.abs(x)))


def _lane_scan(x, op, fill):
    lane = lax.broadcasted_iota(jnp.int32, x.shape, 1)
    d = 1
    while d < x.shape[1]:
        x = op(x, jnp.where(lane >= d, pltpu.roll(x, d, 1), fill))
        d *= 2
    return x


def _mlstm_gate_tables(gr, bias, nc):
    L = MLSTM_L
    i_rows = _softcap(gr[0] + bias[0:1, :])
    f_rows = _log_sigmoid(_softcap(gr[1] + bias[1:2, :]))
    b_rows = _lane_scan(f_rows, jnp.add, 0.0)
    a_rows = i_rows - b_rows
    pm_rows = _lane_scan(a_rows, jnp.maximum, -jnp.inf)
    b_last = b_rows[:, L - 1:L]
    a_max = pm_rows[:, L - 1:L]

    m = jnp.zeros((1, 1), F32)
    ms = [m]
    for c in range(nc):
        m = b_last[c:c + 1, :] + jnp.maximum(m, a_max[c:c + 1, :])
        ms.append(m)
    m_in = jnp.concatenate(ms[:nc], axis=0)
    m_out = jnp.concatenate(ms[1:], axis=0)
    big_m_rows = jnp.maximum(m_in, pm_rows)
    decay = jnp.exp(b_last + m_in - m_out)

    stacked = jnp.concatenate([
        big_m_rows,
        jnp.exp(m_in - big_m_rows),
        jnp.exp(-(b_rows + big_m_rows)),
        jnp.exp(a_rows + b_last - m_out),
        jnp.zeros((L - 4 * nc, L), F32)], axis=0)
    return a_rows, stacked.T, decay


def _mlstm_kernel(q_ref, k_ref, v_ref, o_ref, gr_ref, br_ref, wn_ref, *rest, seq, n_cast):
    cast_src, out_ref, cast_dst = rest[:n_cast], rest[n_cast], rest[n_cast + 1:]
    L = MLSTM_L
    nc = seq // L
    dk, dv = M_QK_DIM, M_V_DIM
    heads = range(MLSTM_HEADS)
    _cast_blocks(cast_src, cast_dst)
    tables = [_mlstm_gate_tables(gr_ref[hh], br_ref[hh], nc) for hh in heads]

    row_t = lax.broadcasted_iota(jnp.int32, (L, L), 0)
    col_s = lax.broadcasted_iota(jnp.int32, (L, L), 1)
    causal = row_t >= col_s
    ones_blk = jnp.ones((L, LANES), BF16)

    states = [jnp.zeros((dk, dv + LANES), F32) for _ in heads]
    for c in range(nc):
        rows = slice(c * L, (c + 1) * L)
        for hh in heads:
            a_rows, cols, decay = tables[hh]
            qb = q_ref[rows, hh * dk:(hh + 1) * dk]
            kb = k_ref[rows, hh * dk:(hh + 1) * dk]
            v_aug = jnp.concatenate([v_ref[rows, hh * dv:(hh + 1) * dv], ones_blk], axis=1)
            big_m = cols[:, c:c + 1]
            w_inter = cols[:, nc + c:nc + c + 1]
            e_neg_m = cols[:, 2 * nc + c:2 * nc + c + 1]
            w_key = cols[:, 3 * nc + c:3 * nc + c + 1]

            e = jnp.exp(jnp.where(causal, a_rows[c:c + 1, :] - big_m, -jnp.inf))
            s = lax.dot_general(qb, kb, NT_DIMS, preferred_element_type=F32) * e
            nd = (w_inter * jnp.dot(qb, states[hh].astype(BF16), preferred_element_type=F32)
                  + jnp.dot(s.astype(BF16), v_aug, preferred_element_type=F32))
            den = nd[:, dv:]
            r = 1.0 / jnp.maximum(jnp.abs(den), e_neg_m)
            h = nd[:, :dv] * jnp.concatenate([r] * (dv // LANES), axis=1)

            hn = _rms_scale(h) * wn_ref[hh]
            gate = jax.nn.sigmoid(o_ref[rows, hh * dv:(hh + 1) * dv])
            out_ref[rows, hh * dv:(hh + 1) * dv] = (hn * gate).astype(out_ref.dtype)

            kw_t = (kb.astype(F32) * w_key).T.astype(BF16)
            states[hh] = decay[c:c + 1, :] * states[hh] + jnp.dot(kw_t, v_aug, preferred_element_type=F32)


def _mlstm(qkv, o_gate, gates_r, bias_r, w_hnorm, cast_weights, batch, seq):
    nc = seq // MLSTM_L
    g = MLSTM_HEADS
    groups = M_HEADS // g
    cast_in, cast_out, cast_shapes = _cast_specs(cast_weights, batch * groups, lambda b, h: b * groups + h)
    qk_w, v_w = g * M_QK_DIM, g * M_V_DIM
    k_off = M_HEADS * M_QK_DIM // qk_w
    v_off = 2 * M_HEADS * M_QK_DIM // v_w
    return pl.pallas_call(
        functools.partial(_mlstm_kernel, seq=seq, n_cast=len(cast_weights)),
        grid=(batch, groups),
        in_specs=[
            pl.BlockSpec((seq, qk_w), lambda b, h: (b, h)),
            pl.BlockSpec((seq, qk_w), lambda b, h: (b, k_off + h)),
            pl.BlockSpec((seq, v_w), lambda b, h: (b, v_off + h)),
            pl.BlockSpec((seq, v_w), lambda b, h: (b, h)),
            pl.BlockSpec((None, g, 2, nc, MLSTM_L), lambda b, h: (b, h, 0, 0, 0)),
            pl.BlockSpec((g, 2, 1), lambda b, h: (h, 0, 0)),
            pl.BlockSpec((g, 1, M_V_DIM), lambda b, h: (h, 0, 0)),
        ] + cast_in,
        out_specs=[pl.BlockSpec((seq, v_w), lambda b, h: (b, h))] + cast_out,
        out_shape=[jax.ShapeDtypeStruct((batch * seq, M_HEADS * M_V_DIM), BF16)] + cast_shapes,
        compiler_params=_params(("parallel", "parallel")),
        name="mlstm_chunkwise",
    )(qkv, qkv, qkv, o_gate, gates_r, bias_r, w_hnorm, *[w for w, _ in cast_weights])


def _matmul_residual_kernel(a_ref, w_ref, r_ref, o_ref, *, tn):
    a = a_ref[...]
    for j in range(w_ref.shape[1] // tn):
        cols = slice(j * tn, (j + 1) * tn)
        o_ref[:, cols] = r_ref[:, cols] + jnp.dot(a, w_ref[:, cols], preferred_element_type=F32)


def _matmul_residual(a, w, res, tm, tn, name):
    t, kdim = a.shape
    n = w.shape[1]
    return pl.pallas_call(
        functools.partial(_matmul_residual_kernel, tn=tn),
        grid=(t // tm,),
        in_specs=[
            pl.BlockSpec((tm, kdim), lambda i: (i, 0)),
            pl.BlockSpec((kdim, n), lambda i: (0, 0)),
            pl.BlockSpec((tm, n), lambda i: (i, 0)),
        ],
        out_specs=pl.BlockSpec((tm, n), lambda i: (i, 0)),
        out_shape=jax.ShapeDtypeStruct((t, n), F32),
        compiler_params=_params(("parallel",)),
        name=name,
    )(a, w, res)


def _conv_ffn_kernel(h_ref, g_ref, wg_ref, wv_ref, cwg_ref, cwv_ref, cbg_ref, cbv_ref, wd_ref,
                     gf_ref, o_ref, xn_ref, tail_ref, *, tiles_per_seq, final_norm):
    i = pl.program_id(0)
    f = pl.program_id(1)
    tm = h_ref.shape[0]

    @pl.when(i % tiles_per_seq == 0)
    def _():
        tail_ref[f] = jnp.zeros(tail_ref.shape[1:], F32)

    def step(first):
        if first:
            xn_ref[...] = (_rms_scale(h_ref[...]) * g_ref[...]).astype(BF16)
        history = [tail_ref[f, 0], tail_ref[f, 1]]
        for r in range(tm // FFN_ROWS):
            rows = slice(r * FFN_ROWS, (r + 1) * FFN_ROWS)
            xn = xn_ref[rows, :]

            def conv(w_ref, cw_ref, cb_ref, slot):
                u = jnp.dot(xn, w_ref[...], preferred_element_type=F32)
                ue = jnp.concatenate([history[slot], u], axis=0)
                history[slot] = u[FFN_ROWS - FFN_TAIL:, :]
                cw = cw_ref[...]
                c = (cb_ref[...] + pltpu.roll(ue, 2, 0) * cw[0:1, :] + pltpu.roll(ue, 1, 0) * cw[1:2, :]
                     + ue * cw[2:3, :])
                return c[FFN_TAIL:, :]

            gate = conv(wg_ref, cwg_ref, cbg_ref, 0)
            val = conv(wv_ref, cwv_ref, cbv_ref, 1)
            act = (gate * jax.nn.sigmoid(gate) * val).astype(BF16)
            base = h_ref[rows, :] if first else o_ref[rows, :]
            o_ref[rows, :] = base + jnp.dot(act, wd_ref[...], preferred_element_type=F32)
        tail_ref[f, 0] = history[0]
        tail_ref[f, 1] = history[1]

    @pl.when(f == 0)
    def _():
        step(True)

    @pl.when(f > 0)
    def _():
        step(False)

    if final_norm:
        @pl.when(f == pl.num_programs(1) - 1)
        def _():
            o_ref[...] = _rms_scale(o_ref[...]) * gf_ref[...]


def _conv_ffn(h, layer, g, w_up, conv_w, conv_b, w_down, g_final, seq, final_norm, name):
    t, d = h.shape
    d_ff = w_down.shape[0]
    tm, tf = FFN_TM, FFN_TF
    nf = d_ff // tf
    kern = functools.partial(_conv_ffn_kernel, tiles_per_seq=seq // tm, final_norm=final_norm)
    return pl.pallas_call(
        kern,
        grid=(t // tm, nf),
        in_specs=[
            pl.BlockSpec((tm, d), lambda i, f: (i, 0)),
            pl.BlockSpec((None, 1, d), lambda i, f: (layer, 0, 0)),
            pl.BlockSpec((d, tf), lambda i, f: (0, f)),
            pl.BlockSpec((d, tf), lambda i, f: (0, nf + f)),
            pl.BlockSpec((None, CONV_W, tf), lambda i, f: (layer, 0, f)),
            pl.BlockSpec((None, CONV_W, tf), lambda i, f: (layer, 0, nf + f)),
            pl.BlockSpec((None, 1, tf), lambda i, f: (layer, 0, f)),
            pl.BlockSpec((None, 1, tf), lambda i, f: (layer, 0, nf + f)),
            pl.BlockSpec((tf, d), lambda i, f: (f, 0)),
            pl.BlockSpec((1, d), lambda i, f: (0, 0)),
        ],
        out_specs=pl.BlockSpec((tm, d), lambda i, f: (i, 0)),
        out_shape=jax.ShapeDtypeStruct((t, d), F32),
        scratch_shapes=[
            pltpu.VMEM((tm, d), BF16),
            pltpu.VMEM((nf, 2, FFN_TAIL, tf), F32),
        ],
        compiler_params=_params(("arbitrary", "arbitrary")),
        name=name,
    )(h, g, w_up, w_up, conv_w, conv_w, conv_b, conv_b, w_down, g_final)


def _rope(y, cos_t, sin_lo, sin_hi):
    outs = []
    for gidx in range(y.shape[1] // LANES):
        blk = y[:, gidx * LANES:(gidx + 1) * LANES]
        half = ROPE_DIM // 2
        outs.append(blk * cos_t + pltpu.roll(blk, half, 1) * sin_hi + pltpu.roll(blk, LANES - half, 1) * sin_lo)
    return jnp.concatenate(outs, axis=1)


def _kvq_kernel(x_ref, pos_ref, invf_ref, gkv_ref, gq_ref, wkv_ref, wq_ref, o_ref,
                xkv_ref, xq_ref, cos_ref, slo_ref, shi_ref, *, k_tiles, kv_tiles):
    j = pl.program_id(1)

    def prologue():
        y = _rms_scale(x_ref[...])
        xkv_ref[...] = (y * gkv_ref[...]).astype(BF16)
        xq_ref[...] = (y * gq_ref[...]).astype(BF16)
        half = ROPE_DIM // 2
        ang = invf_ref[...] * pos_ref[...].astype(F32)
        cos = jnp.cos(ang)
        sin = jnp.sin(ang)
        pad = LANES - ROPE_DIM
        tm = ang.shape[1]
        cos_ref[...] = jnp.concatenate([cos, cos, jnp.ones((pad, tm), F32)], axis=0).T
        sin_t = jnp.concatenate([-sin, sin, jnp.zeros((pad, tm), F32)], axis=0).T
        lane = lax.broadcasted_iota(jnp.int32, sin_t.shape, 1)
        slo_ref[...] = jnp.where(lane < half, sin_t, 0.0)
        shi_ref[...] = jnp.where(lane >= half, sin_t, 0.0)

    halves = [slice(r * KVQ_ROWS, (r + 1) * KVQ_ROWS) for r in range(x_ref.shape[0] // KVQ_ROWS)]

    def rotated(xn_ref, w_ref, rows, scale):
        y = jnp.dot(xn_ref[rows, :], w_ref[...], preferred_element_type=F32)
        y = _rope(y, cos_ref[rows, :], slo_ref[rows, :], shi_ref[rows, :])
        return y if scale is None else y * scale

    def k_step():
        for rows in halves:
            o_ref[rows, :] = rotated(xkv_ref, wkv_ref, rows, None).astype(BF16)

    @pl.when(j == 0)
    def _():
        prologue()
        k_step()

    @pl.when((j > 0) & (j < k_tiles))
    def _():
        k_step()

    @pl.when((j >= k_tiles) & (j < kv_tiles))
    def _():
        o_ref[...] = jnp.dot(xkv_ref[...], wkv_ref[...], preferred_element_type=F32).astype(BF16)

    @pl.when(j >= kv_tiles)
    def _():
        for rows in halves:
            o_ref[rows, :] = rotated(xq_ref, wq_ref, rows, Q_SCALE_LOG2).astype(BF16)


def _kvq_proj(h, pos_rows, inv_freq_col, g_kv, g_q, w_kv, w_q, k_cols):
    t, d = h.shape
    tm, tn = PROJ_TM, KVQ_TN
    k_tiles = k_cols // tn
    kv_tiles = w_kv.shape[1] // tn
    q_tiles = w_q.shape[1] // tn
    kern = functools.partial(_kvq_kernel, k_tiles=k_tiles, kv_tiles=kv_tiles)
    return pl.pallas_call(
        kern,
        grid=(t // tm, kv_tiles + q_tiles),
        in_specs=[
            pl.BlockSpec((tm, d), lambda i, j: (i, 0)),
            pl.BlockSpec((None, 1, tm), lambda i, j: (i, 0, 0)),
            pl.BlockSpec((ROPE_DIM // 2, 1), lambda i, j: (0, 0)),
            pl.BlockSpec((1, d), lambda i, j: (0, 0)),
            pl.BlockSpec((1, d), lambda i, j: (0, 0)),
            pl.BlockSpec((d, tn), lambda i, j: (0, jnp.minimum(j, kv_tiles - 1))),
            pl.BlockSpec((d, tn), lambda i, j: (0, jnp.maximum(j - kv_tiles, 0))),
        ],
        out_specs=pl.BlockSpec((tm, tn), lambda i, j: (i, j)),
        out_shape=jax.ShapeDtypeStruct((t, (kv_tiles + q_tiles) * tn), BF16),
        scratch_shapes=[
            pltpu.VMEM((tm, d), BF16),
            pltpu.VMEM((tm, d), BF16),
            pltpu.VMEM((tm, LANES), F32),
            pltpu.VMEM((tm, LANES), F32),
            pltpu.VMEM((tm, LANES), F32),
        ],
        compiler_params=_params(("parallel", "arbitrary")),
        name="norm_kvq_rope",
    )(h, pos_rows, inv_freq_col, g_kv, g_q, w_kv, w_q)


def _diff_attn_kernel(k_ref, v_ref, q_ref, lam_ref, g_ref, *rest, lambda_init, seq, n_cast):
    cast_src, o_ref, cast_dst = rest[:n_cast], rest[n_cast], rest[n_cast + 1:]
    tb = ATT_T
    _cast_blocks(cast_src, cast_dst)
    dk = A_QK_DIM
    nblk = seq // tb

    lv = lam_ref[...]
    lam = (jnp.exp(jnp.sum(lv[0:1, :] * lv[1:2, :], axis=1, keepdims=True))
           - jnp.exp(jnp.sum(lv[2:3, :] * lv[3:4, :], axis=1, keepdims=True)) + lambda_init)
    g_col = g_ref[...] * (1.0 - lambda_init)
    key_idx = lax.broadcasted_iota(jnp.int32, (tb, tb), 0)
    qry_idx = lax.broadcasted_iota(jnp.int32, (tb, tb), 1)
    diag_mask = key_idx <= qry_idx
    v_t = [v_ref[kb * tb:(kb + 1) * tb, :].astype(F32).T.astype(BF16) for kb in range(nblk)]

    for qi in range(nblk):
        q = q_ref[qi * tb:(qi + 1) * tb, :]
        comps = []
        for c in range(2):
            qc = q[:, c * dk:(c + 1) * dk]
            m = l = acc = None
            for kb in range(qi + 1):
                kc = k_ref[kb * tb:(kb + 1) * tb, c * dk:(c + 1) * dk]
                s = lax.dot_general(kc, qc, NT_DIMS, preferred_element_type=F32)
                if kb == qi:
                    s = jnp.where(diag_mask, s, -jnp.inf)
                s_max = jnp.max(s, axis=0, keepdims=True)
                if kb == 0:
                    m = s_max
                    p = jnp.exp2(s - m)
                    l = jnp.sum(p, axis=0, keepdims=True)
                    acc = jnp.dot(v_t[kb], p.astype(BF16), preferred_element_type=F32)
                else:
                    m_new = jnp.maximum(m, s_max)
                    alpha = jnp.exp2(m - m_new)
                    p = jnp.exp2(s - m_new)
                    l = alpha * l + jnp.sum(p, axis=0, keepdims=True)
                    acc = alpha * acc + jnp.dot(v_t[kb], p.astype(BF16), preferred_element_type=F32)
                    m = m_new
            comps.append(acc * (1.0 / l))
        o_t = comps[0] - lam * comps[1]
        o_t = o_t * lax.rsqrt(jnp.mean(o_t * o_t, axis=0, keepdims=True) + EPS) * g_col
        o_ref[qi * tb:(qi + 1) * tb, :] = o_t.T.astype(o_ref.dtype)


def _diff_attention(kvq, lam_vecs, g_subln_col, cast_weights, batch, seq, lambda_init):
    hv = A_V_DIM
    cast_in, cast_out, cast_shapes = _cast_specs(cast_weights, batch * A_HEADS, lambda b, h: b * A_HEADS + h)
    return pl.pallas_call(
        functools.partial(_diff_attn_kernel, lambda_init=lambda_init, seq=seq, n_cast=len(cast_weights)),
        grid=(batch, A_HEADS),
        in_specs=[
            pl.BlockSpec((seq, hv), lambda b, h: (b, h)),
            pl.BlockSpec((seq, hv), lambda b, h: (b, A_HEADS + h)),
            pl.BlockSpec((seq, hv), lambda b, h: (b, 2 * A_HEADS + h)),
            pl.BlockSpec((4, A_QK_DIM), lambda b, h: (0, 0)),
            pl.BlockSpec((hv, 1), lambda b, h: (0, 0)),
        ] + cast_in,
        out_specs=[pl.BlockSpec((seq, hv), lambda b, h: (b, h))] + cast_out,
        out_shape=[jax.ShapeDtypeStruct((batch * seq, A_HEADS * hv), BF16)] + cast_shapes,
        compiler_params=_params(("parallel", "parallel")),
        name="diff_attention",
    )(kvq, kvq, kvq, lam_vecs, g_subln_col, *[w for w, _ in cast_weights])


def kernel(x, positions, a_norm, m_w_in, m_b_igate, m_b_fgate, m_w_hnorm, m_w_out, kv_norm, w_kv, b_norm, w_q, lam_q1, lam_k1, lam_q2, lam_k2, subln, w_o, f_norm, w_up, conv_w, conv_b, w_down, final_norm):
    batch, seq, d = x.shape
    t = batch * seq
    depth = f_norm.shape[0]
    assert depth == 2 and a_norm.shape[0] == 1 and b_norm.shape[0] == 1
    assert seq % FFN_TM == 0 and seq % ATT_T == 0 and t % PROJ_TM == 0
    assert seq % MLSTM_L == 0 and 4 * (seq // MLSTM_L) <= MLSTM_L

    h = x.reshape(t, d)

    qkv_cols = 2 * M_HEADS * M_QK_DIM + M_HEADS * M_V_DIM
    o_cols = M_HEADS * M_V_DIM
    w_in = m_w_in[0].astype(BF16)
    w_gates = jnp.pad(m_w_in[0][:, qkv_cols + o_cols:], ((0, 0), (0, LANES - 2 * M_HEADS))).astype(BF16)
    qkv, o_gate, gates = _w_in_proj(h, a_norm[0][None, :], w_in, w_gates, qkv_cols, o_cols)

    nc = seq // MLSTM_L
    gates = gates[:, :2 * M_HEADS].reshape(batch, nc, MLSTM_L, 2, M_HEADS)
    gates_r = gates.transpose(0, 4, 3, 1, 2)
    bias = jnp.stack([m_b_igate[0], m_b_fgate[0]], axis=1)
    hg, w_up_0, w_down_0, w_out_b, w_q_b = _mlstm(
        qkv, o_gate, gates_r, bias[:, :, None], m_w_hnorm[0][:, None, :],
        ((w_up, 0), (w_down, 0), (m_w_out, 0), (w_q, 0)), batch, seq)
    h = _matmul_residual(hg, w_out_b, h, RES_TM, RES_TN, "w_out_residual")

    f_gain = f_norm[:, None, :]
    conv_b3 = conv_b[:, None, :]
    h = _conv_ffn(h, 0, f_gain, w_up_0, conv_w, conv_b3, w_down_0, final_norm[None, :], seq, False, "conv_ffn_0")

    layer = 1
    lambda_init = 0.8 - 0.6 * math.exp(-0.3 * layer)
    half = ROPE_DIM // 2
    inv_freq_col = (ROPE_THETA ** (-jnp.arange(half, dtype=F32) / half))[:, None]
    k_cols = A_HEADS * 2 * A_QK_DIM
    pos_rows = positions.reshape(t // PROJ_TM, 1, PROJ_TM)
    kvq = _kvq_proj(h, pos_rows, inv_freq_col, kv_norm[None, :], b_norm[0][None, :],
                    w_kv.astype(BF16), w_q_b, k_cols)
    lam_vecs = jnp.stack([lam_q1[0], lam_k1[0], lam_q2[0], lam_k2[0]], axis=0)
    att, w_up_1, w_down_1, w_o_b = _diff_attention(kvq, lam_vecs, subln[0][:, None],
                                                   ((w_up, 1), (w_down, 1), (w_o, 0)), batch, seq, lambda_init)
    h = _matmul_residual(att, w_o_b, h, RES_TM, RES_TN, "w_o_residual")

    h = _conv_ffn(h, 1, f_gain, w_up_1, conv_w, conv_b3, w_down_1, final_norm[None, :], seq, True, "conv_ffn_1")
    return h.reshape(batch, seq, d)
```

```python
import functools
import math

import jax
import jax.numpy as jnp
from jax import lax
from jax.experimental import pallas as pl
from jax.experimental.pallas import tpu as pltpu

F32 = jnp.float32
BF16 = jnp.bfloat16

EPS = 1e-6
M_HEADS = 8
M_QK_DIM = 128
M_V_DIM = 256
GATE_SOFTCAP = 15.0
A_HEADS = 8
A_QK_DIM = 128
A_V_DIM = 256
ROPE_DIM = 32
ROPE_THETA = 500000.0
CONV_W = 3

LANES = 128
V7X_VMEM_LIMIT_BYTES = 56 * 1024 * 1024

PROJ_TM = 1024
W_IN_TN = 1024
KVQ_TN = 1024
KVQ_ROWS = 512
RES_TM = 512
RES_TN = 512
FFN_TM = 1024
FFN_ROWS = 512
FFN_TF = 512
FFN_TAIL = 8
ATT_T = 256
MLSTM_L = LANES
MLSTM_HEADS = 2

NT_DIMS = (((1,), (1,)), ((), ()))
Q_SCALE_LOG2 = (A_QK_DIM ** -0.5) * math.log2(math.e)


def _rms_scale(x):
    return x * lax.rsqrt(jnp.mean(x * x, axis=-1, keepdims=True) + EPS)


def _params(semantics):
    return pltpu.CompilerParams(dimension_semantics=semantics, vmem_limit_bytes=V7X_VMEM_LIMIT_BYTES)


def _cast_specs(weights, steps, step_of):
    in_specs, out_specs, out_shapes = [], [], []
    for w, layer in weights:
        rows, cols = w.shape[1:]
        assert rows % steps == 0
        blk = rows // steps
        in_specs.append(pl.BlockSpec((None, blk, cols), lambda *idx, layer=layer: (layer, step_of(*idx), 0)))
        out_specs.append(pl.BlockSpec((blk, cols), lambda *idx: (step_of(*idx), 0)))
        out_shapes.append(jax.ShapeDtypeStruct((rows, cols), BF16))
    return in_specs, out_specs, out_shapes


def _cast_blocks(src_refs, dst_refs):
    for src, dst in zip(src_refs, dst_refs):
        dst[...] = src[...].astype(BF16)


def _w_in_kernel(x_ref, g_ref, w_ref, wg_ref, qkv_ref, o_ref, gates_ref, xn_ref, *, qkv_tiles):
    j = pl.program_id(1)

    @pl.when(j == 0)
    def _():
        xn = (_rms_scale(x_ref[...]) * g_ref[...]).astype(BF16)
        xn_ref[...] = xn
        gates_ref[...] = jnp.dot(xn, wg_ref[...], preferred_element_type=F32)
        y = jnp.dot(xn, w_ref[...], preferred_element_type=F32)
        qkv_ref[...] = (y * (M_QK_DIM ** -0.5)).astype(BF16)

    @pl.when((j > 0) & (j < qkv_tiles))
    def _():
        qkv_ref[...] = jnp.dot(xn_ref[...], w_ref[...], preferred_element_type=F32).astype(BF16)

    @pl.when(j >= qkv_tiles)
    def _():
        o_ref[...] = jnp.dot(xn_ref[...], w_ref[...], preferred_element_type=F32)


def _w_in_proj(x, g, w, w_gates, qkv_cols, o_cols):
    t, d = x.shape
    tm, tn = PROJ_TM, W_IN_TN
    assert M_HEADS * M_QK_DIM == tn and qkv_cols % tn == 0 and o_cols % tn == 0
    qkv_tiles = qkv_cols // tn
    return pl.pallas_call(
        functools.partial(_w_in_kernel, qkv_tiles=qkv_tiles),
        grid=(t // tm, (qkv_cols + o_cols) // tn),
        in_specs=[
            pl.BlockSpec((tm, d), lambda i, j: (i, 0)),
            pl.BlockSpec((1, d), lambda i, j: (0, 0)),
            pl.BlockSpec((d, tn), lambda i, j: (0, j)),
            pl.BlockSpec((d, LANES), lambda i, j: (0, 0)),
        ],
        out_specs=[
            pl.BlockSpec((tm, tn), lambda i, j: (i, jnp.minimum(j, qkv_tiles - 1))),
            pl.BlockSpec((tm, tn), lambda i, j: (i, jnp.maximum(j - qkv_tiles, 0))),
            pl.BlockSpec((tm, LANES), lambda i, j: (i, 0)),
        ],
        out_shape=[
            jax.ShapeDtypeStruct((t, qkv_cols), BF16),
            jax.ShapeDtypeStruct((t, o_cols), F32),
            jax.ShapeDtypeStruct((t, LANES), F32),
        ],
        scratch_shapes=[pltpu.VMEM((tm, d), BF16)],
        compiler_params=_params(("parallel", "arbitrary")),
        name="norm_w_in",
    )(x, g, w, w_gates)


def _softcap(t):
    return GATE_SOFTCAP * jnp.tanh(t / GATE_SOFTCAP)


def _log_sigmoid(x):
    return jnp.minimum(x, 0.0) - jnp.log1p(jnp.exp(-jnp.abs(x)))


def _lane_scan(x, op, fill):
    lane = lax.broadcasted_iota(jnp.int32, x.shape, 1)
    d = 1
    while d < x.shape[1]:
        x = op(x, jnp.where(lane >= d, pltpu.roll(x, d, 1), fill))
        d *= 2
    return x


def _mlstm_gate_tables(gr, bias, nc):
    L = MLSTM_L
    i_rows = _softcap(gr[0] + bias[0:1, :])
    f_rows = _log_sigmoid(_softcap(gr[1] + bias[1:2, :]))
    b_rows = _lane_scan(f_rows, jnp.add, 0.0)
    a_rows = i_rows - b_rows
    pm_rows = _lane_scan(a_rows, jnp.maximum, -jnp.inf)
    b_last = b_rows[:, L - 1:L]
    a_max = pm_rows[:, L - 1:L]

    m = jnp.zeros((1, 1), F32)
    ms = [m]
    for c in range(nc):
        m = b_last[c:c + 1, :] + jnp.maximum(m, a_max[c:c + 1, :])
        ms.append(m)
    m_in = jnp.concatenate(ms[:nc], axis=0)
    m_out = jnp.concatenate(ms[1:], axis=0)
    big_m_rows = jnp.maximum(m_in, pm_rows)
    decay = jnp.exp(b_last + m_in - m_out)

    stacked = jnp.concatenate([
        big_m_rows,
        jnp.exp(m_in - big_m_rows),
        jnp.exp(-(b_rows + big_m_rows)),
        jnp.exp(a_rows + b_last - m_out),
        jnp.zeros((L - 4 * nc, L), F32)], axis=0)
    return a_rows, stacked.T, decay


def _mlstm_kernel(q_ref, k_ref, v_ref, o_ref, gr_ref, br_ref, wn_ref, *rest, seq, n_cast):
    cast_src, out_ref, cast_dst = rest[:n_cast], rest[n_cast], rest[n_cast + 1:]
    L = MLSTM_L
    nc = seq // L
    dk, dv = M_QK_DIM, M_V_DIM
    heads = range(MLSTM_HEADS)
    _cast_blocks(cast_src, cast_dst)
    tables = [_mlstm_gate_tables(gr_ref[hh], br_ref[hh], nc) for hh in heads]

    row_t = lax.broadcasted_iota(jnp.int32, (L, L), 0)
    col_s = lax.broadcasted_iota(jnp.int32, (L, L), 1)
    causal = row_t >= col_s
    ones_blk = jnp.ones((L, LANES), BF16)

    states = [jnp.zeros((dk, dv + LANES), F32) for _ in heads]
    for c in range(nc):
        rows = slice(c * L, (c + 1) * L)
        for hh in heads:
            a_rows, cols, decay = tables[hh]
            qb = q_ref[rows, hh * dk:(hh + 1) * dk]
            kb = k_ref[rows, hh * dk:(hh + 1) * dk]
            v_aug = jnp.concatenate([v_ref[rows, hh * dv:(hh + 1) * dv], ones_blk], axis=1)
            big_m = cols[:, c:c + 1]
            w_inter = cols[:, nc + c:nc + c + 1]
            e_neg_m = cols[:, 2 * nc + c:2 * nc + c + 1]
            w_key = cols[:, 3 * nc + c:3 * nc + c + 1]

            e = jnp.exp(jnp.where(causal, a_rows[c:c + 1, :] - big_m, -jnp.inf))
            s = lax.dot_general(qb, kb, NT_DIMS, preferred_element_type=F32) * e
            nd = (w_inter * jnp.dot(qb, states[hh].astype(BF16), preferred_element_type=F32)
                  + jnp.dot(s.astype(BF16), v_aug, preferred_element_type=F32))
            den = nd[:, dv:]
            r = 1.0 / jnp.maximum(jnp.abs(den), e_neg_m)
            h = nd[:, :dv] * jnp.concatenate([r] * (dv // LANES), axis=1)

            hn = _rms_scale(h) * wn_ref[hh]
            gate = jax.nn.sigmoid(o_ref[rows, hh * dv:(hh + 1) * dv])
            out_ref[rows, hh * dv:(hh + 1) * dv] = (hn * gate).astype(out_ref.dtype)

            kw_t = (kb.astype(F32) * w_key).T.astype(BF16)
            states[hh] = decay[c:c + 1, :] * states[hh] + jnp.dot(kw_t, v_aug, preferred_element_type=F32)


def _mlstm(qkv, o_gate, gates_r, bias_r, w_hnorm, cast_weights, batch, seq):
    nc = seq // MLSTM_L
    g = MLSTM_HEADS
    groups = M_HEADS // g
    cast_in, cast_out, cast_shapes = _cast_specs(cast_weights, batch * groups, lambda b, h: b * groups + h)
    qk_w, v_w = g * M_QK_DIM, g * M_V_DIM
    k_off = M_HEADS * M_QK_DIM // qk_w
    v_off = 2 * M_HEADS * M_QK_DIM // v_w
    return pl.pallas_call(
        functools.partial(_mlstm_kernel, seq=seq, n_cast=len(cast_weights)),
        grid=(batch, groups),
        in_specs=[
            pl.BlockSpec((seq, qk_w), lambda b, h: (b, h)),
            pl.BlockSpec((seq, qk_w), lambda b, h: (b, k_off + h)),
            pl.BlockSpec((seq, v_w), lambda b, h: (b, v_off + h)),
            pl.BlockSpec((seq, v_w), lambda b, h: (b, h)),
            pl.BlockSpec((None, g, 2, nc, MLSTM_L), lambda b, h: (b, h, 0, 0, 0)),
            pl.BlockSpec((g, 2, 1), lambda b, h: (h, 0, 0)),
            pl.BlockSpec((g, 1, M_V_DIM), lambda b, h: (h, 0, 0)),
        ] + cast_in,
        out_specs=[pl.BlockSpec((seq, v_w), lambda b, h: (b, h))] + cast_out,
        out_shape=[jax.ShapeDtypeStruct((batch * seq, M_HEADS * M_V_DIM), BF16)] + cast_shapes,
        compiler_params=_params(("parallel", "parallel")),
        name="mlstm_chunkwise",
    )(qkv, qkv, qkv, o_gate, gates_r, bias_r, w_hnorm, *[w for w, _ in cast_weights])


def _matmul_residual_kernel(a_ref, w_ref, r_ref, o_ref, *, tn):
    a = a_ref[...]
    for j in range(w_ref.shape[1] // tn):
        cols = slice(j * tn, (j + 1) * tn)
        o_ref[:, cols] = r_ref[:, cols] + jnp.dot(a, w_ref[:, cols], preferred_element_type=F32)


def _matmul_residual(a, w, res, tm, tn, name):
    t, kdim = a.shape
    n = w.shape[1]
    return pl.pallas_call(
        functools.partial(_matmul_residual_kernel, tn=tn),
        grid=(t // tm,),
        in_specs=[
            pl.BlockSpec((tm, kdim), lambda i: (i, 0)),
            pl.BlockSpec((kdim, n), lambda i: (0, 0)),
            pl.BlockSpec((tm, n), lambda i: (i, 0)),
        ],
        out_specs=pl.BlockSpec((tm, n), lambda i: (i, 0)),
        out_shape=jax.ShapeDtypeStruct((t, n), F32),
        compiler_params=_params(("parallel",)),
        name=name,
    )(a, w, res)


def _conv_ffn_kernel(h_ref, g_ref, wg_ref, wv_ref, cwg_ref, cwv_ref, cbg_ref, cbv_ref, wd_ref,
                     gf_ref, o_ref, xn_ref, tail_ref, u_ref, *, tiles_per_seq, final_norm):
    i = pl.program_id(0)
    f = pl.program_id(1)
    tm = h_ref.shape[0]

    @pl.when(i % tiles_per_seq == 0)
    def _():
        tail_ref[f] = jnp.zeros(tail_ref.shape[1:], F32)

    def step(first):
        if first:
            xn_ref[...] = (_rms_scale(h_ref[...]) * g_ref[...]).astype(BF16)
        history = [tail_ref[f, 0], tail_ref[f, 1]]
        for r in range(tm // FFN_ROWS):
            rows = slice(r * FFN_ROWS, (r + 1) * FFN_ROWS)
            xn = xn_ref[rows, :]

            def conv(w_ref, cw_ref, cb_ref, slot):
                u_ref[slot, :FFN_TAIL, :] = history[slot]
                u_ref[slot, FFN_TAIL:, :] = jnp.dot(xn, w_ref[...], preferred_element_type=F32)
                ue = u_ref[slot]
                history[slot] = ue[FFN_ROWS:, :]
                cw = cw_ref[...]
                c = (cb_ref[...] + pltpu.roll(ue, 2, 0) * cw[0:1, :] + pltpu.roll(ue, 1, 0) * cw[1:2, :]
                     + ue * cw[2:3, :])
                return c[FFN_TAIL:, :]

            gate = conv(wg_ref, cwg_ref, cbg_ref, 0)
            val = conv(wv_ref, cwv_ref, cbv_ref, 1)
            act = (gate * jax.nn.sigmoid(gate) * val).astype(BF16)
            base = h_ref[rows, :] if first else o_ref[rows, :]
            o_ref[rows, :] = base + jnp.dot(act, wd_ref[...], preferred_element_type=F32)
        tail_ref[f, 0] = history[0]
        tail_ref[f, 1] = history[1]

    @pl.when(f == 0)
    def _():
        step(True)

    @pl.when(f > 0)
    def _():
        step(False)

    if final_norm:
        @pl.when(f == pl.num_programs(1) - 1)
        def _():
            o_ref[...] = _rms_scale(o_ref[...]) * gf_ref[...]


def _conv_ffn(h, layer, g, w_up, conv_w, conv_b, w_down, g_final, seq, final_norm, name):
    t, d = h.shape
    d_ff = w_down.shape[0]
    tm, tf = FFN_TM, FFN_TF
    nf = d_ff // tf
    kern = functools.partial(_conv_ffn_kernel, tiles_per_seq=seq // tm, final_norm=final_norm)
    return pl.pallas_call(
        kern,
        grid=(t // tm, nf),
        in_specs=[
            pl.BlockSpec((tm, d), lambda i, f: (i, 0)),
            pl.BlockSpec((None, 1, d), lambda i, f: (layer, 0, 0)),
            pl.BlockSpec((d, tf), lambda i, f: (0, f)),
            pl.BlockSpec((d, tf), lambda i, f: (0, nf + f)),
            pl.BlockSpec((None, CONV_W, tf), lambda i, f: (layer, 0, f)),
            pl.BlockSpec((None, CONV_W, tf), lambda i, f: (layer, 0, nf + f)),
            pl.BlockSpec((None, 1, tf), lambda i, f: (layer, 0, f)),
            pl.BlockSpec((None, 1, tf), lambda i, f: (layer, 0, nf + f)),
            pl.BlockSpec((tf, d), lambda i, f: (f, 0)),
            pl.BlockSpec((1, d), lambda i, f: (0, 0)),
        ],
        out_specs=pl.BlockSpec((tm, d), lambda i, f: (i, 0)),
        out_shape=jax.ShapeDtypeStruct((t, d), F32),
        scratch_shapes=[
            pltpu.VMEM((tm, d), BF16),
            pltpu.VMEM((nf, 2, FFN_TAIL, tf), F32),
            pltpu.VMEM((2, FFN_TAIL + FFN_ROWS, tf), F32),
        ],
        compiler_params=_params(("arbitrary", "arbitrary")),
        name=name,
    )(h, g, w_up, w_up, conv_w, conv_w, conv_b, conv_b, w_down, g_final)


def _rope(y, cos_t, sin_lo, sin_hi):
    outs = []
    for gidx in range(y.shape[1] // LANES):
        blk = y[:, gidx * LANES:(gidx + 1) * LANES]
        half = ROPE_DIM // 2
        outs.append(blk * cos_t + pltpu.roll(blk, half, 1) * sin_hi + pltpu.roll(blk, LANES - half, 1) * sin_lo)
    return jnp.concatenate(outs, axis=1)


def _kvq_kernel(x_ref, pos_ref, invf_ref, gkv_ref, gq_ref, wkv_ref, wq_ref, o_ref,
                xkv_ref, xq_ref, cos_ref, slo_ref, shi_ref, *, k_tiles, kv_tiles):
    j = pl.program_id(1)

    def prologue():
        y = _rms_scale(x_ref[...])
        xkv_ref[...] = (y * gkv_ref[...]).astype(BF16)
        xq_ref[...] = (y * gq_ref[...]).astype(BF16)
        half = ROPE_DIM // 2
        ang = invf_ref[...] * pos_ref[...].astype(F32)
        cos = jnp.cos(ang)
        sin = jnp.sin(ang)
        pad = LANES - ROPE_DIM
        tm = ang.shape[1]
        cos_ref[...] = jnp.concatenate([cos, cos, jnp.ones((pad, tm), F32)], axis=0).T
        sin_t = jnp.concatenate([-sin, sin, jnp.zeros((pad, tm), F32)], axis=0).T
        lane = lax.broadcasted_iota(jnp.int32, sin_t.shape, 1)
        slo_ref[...] = jnp.where(lane < half, sin_t, 0.0)
        shi_ref[...] = jnp.where(lane >= half, sin_t, 0.0)

    halves = [slice(r * KVQ_ROWS, (r + 1) * KVQ_ROWS) for r in range(x_ref.shape[0] // KVQ_ROWS)]

    def rotated(xn_ref, w_ref, rows, scale):
        y = jnp.dot(xn_ref[rows, :], w_ref[...], preferred_element_type=F32)
        y = _rope(y, cos_ref[rows, :], slo_ref[rows, :], shi_ref[rows, :])
        return y if scale is None else y * scale

    def k_step():
        for rows in halves:
            o_ref[rows, :] = rotated(xkv_ref, wkv_ref, rows, None).astype(BF16)

    @pl.when(j == 0)
    def _():
        prologue()
        k_step()

    @pl.when((j > 0) & (j < k_tiles))
    def _():
        k_step()

    @pl.when((j >= k_tiles) & (j < kv_tiles))
    def _():
        o_ref[...] = jnp.dot(xkv_ref[...], wkv_ref[...], preferred_element_type=F32).astype(BF16)

    @pl.when(j >= kv_tiles)
    def _():
        for rows in halves:
            o_ref[rows, :] = rotated(xq_ref, wq_ref, rows, Q_SCALE_LOG2).astype(BF16)


def _kvq_proj(h, pos_rows, inv_freq_col, g_kv, g_q, w_kv, w_q, k_cols):
    t, d = h.shape
    tm, tn = PROJ_TM, KVQ_TN
    k_tiles = k_cols // tn
    kv_tiles = w_kv.shape[1] // tn
    q_tiles = w_q.shape[1] // tn
    kern = functools.partial(_kvq_kernel, k_tiles=k_tiles, kv_tiles=kv_tiles)
    return pl.pallas_call(
        kern,
        grid=(t // tm, kv_tiles + q_tiles),
        in_specs=[
            pl.BlockSpec((tm, d), lambda i, j: (i, 0)),
            pl.BlockSpec((None, 1, tm), lambda i, j: (i, 0, 0)),
            pl.BlockSpec((ROPE_DIM // 2, 1), lambda i, j: (0, 0)),
            pl.BlockSpec((1, d), lambda i, j: (0, 0)),
            pl.BlockSpec((1, d), lambda i, j: (0, 0)),
            pl.BlockSpec((d, tn), lambda i, j: (0, jnp.minimum(j, kv_tiles - 1))),
            pl.BlockSpec((d, tn), lambda i, j: (0, jnp.maximum(j - kv_tiles, 0))),
        ],
        out_specs=pl.BlockSpec((tm, tn), lambda i, j: (i, j)),
        out_shape=jax.ShapeDtypeStruct((t, (kv_tiles + q_tiles) * tn), BF16),
        scratch_shapes=[
            pltpu.VMEM((tm, d), BF16),
            pltpu.VMEM((tm, d), BF16),
            pltpu.VMEM((tm, LANES), F32),
            pltpu.VMEM((tm, LANES), F32),
            pltpu.VMEM((tm, LANES), F32),
        ],
        compiler_params=_params(("parallel", "arbitrary")),
        name="norm_kvq_rope",
    )(h, pos_rows, inv_freq_col, g_kv, g_q, w_kv, w_q)


def _diff_attn_kernel(k_ref, v_ref, q_ref, lam_ref, g_ref, *rest, lambda_init, seq, n_cast):
    cast_src, o_ref, cast_dst = rest[:n_cast], rest[n_cast], rest[n_cast + 1:]
    tb = ATT_T
    _cast_blocks(cast_src, cast_dst)
    dk = A_QK_DIM
    nblk = seq // tb

    lv = lam_ref[...]
    lam = (jnp.exp(jnp.sum(lv[0:1, :] * lv[1:2, :], axis=1, keepdims=True))
           - jnp.exp(jnp.sum(lv[2:3, :] * lv[3:4, :], axis=1, keepdims=True)) + lambda_init)
    g_col = g_ref[...] * (1.0 - lambda_init)
    key_idx = lax.broadcasted_iota(jnp.int32, (tb, tb), 0)
    qry_idx = lax.broadcasted_iota(jnp.int32, (tb, tb), 1)
    diag_mask = key_idx <= qry_idx
    v_t = [v_ref[kb * tb:(kb + 1) * tb, :].astype(F32).T.astype(BF16) for kb in range(nblk)]

    for qi in range(nblk):
        q = q_ref[qi * tb:(qi + 1) * tb, :]
        comps = []
        for c in range(2):
            qc = q[:, c * dk:(c + 1) * dk]
            m = l = acc = None
            for kb in range(qi + 1):
                kc = k_ref[kb * tb:(kb + 1) * tb, c * dk:(c + 1) * dk]
                s = lax.dot_general(kc, qc, NT_DIMS, preferred_element_type=F32)
                if kb == qi:
                    s = jnp.where(diag_mask, s, -jnp.inf)
                s_max = jnp.max(s, axis=0, keepdims=True)
                if kb == 0:
                    m = s_max
                    p = jnp.exp2(s - m)
                    l = jnp.sum(p, axis=0, keepdims=True)
                    acc = jnp.dot(v_t[kb], p.astype(BF16), preferred_element_type=F32)
                else:
                    m_new = jnp.maximum(m, s_max)
                    alpha = jnp.exp2(m - m_new)
                    p = jnp.exp2(s - m_new)
                    l = alpha * l + jnp.sum(p, axis=0, keepdims=True)
                    acc = alpha * acc + jnp.dot(v_t[kb], p.astype(BF16), preferred_element_type=F32)
                    m = m_new
            comps.append(acc * (1.0 / l))
        o_t = comps[0] - lam * comps[1]
        o_t = o_t * lax.rsqrt(jnp.mean(o_t * o_t, axis=0, keepdims=True) + EPS) * g_col
        o_ref[qi * tb:(qi + 1) * tb, :] = o_t.T.astype(o_ref.dtype)


def _diff_attention(kvq, lam_vecs, g_subln_col, cast_weights, batch, seq, lambda_init):
    hv = A_V_DIM
    cast_in, cast_out, cast_shapes = _cast_specs(cast_weights, batch * A_HEADS, lambda b, h: b * A_HEADS + h)
    return pl.pallas_call(
        functools.partial(_diff_attn_kernel, lambda_init=lambda_init, seq=seq, n_cast=len(cast_weights)),
        grid=(batch, A_HEADS),
        in_specs=[
            pl.BlockSpec((seq, hv), lambda b, h: (b, h)),
            pl.BlockSpec((seq, hv), lambda b, h: (b, A_HEADS + h)),
            pl.BlockSpec((seq, hv), lambda b, h: (b, 2 * A_HEADS + h)),
            pl.BlockSpec((4, A_QK_DIM), lambda b, h: (0, 0)),
            pl.BlockSpec((hv, 1), lambda b, h: (0, 0)),
        ] + cast_in,
        out_specs=[pl.BlockSpec((seq, hv), lambda b, h: (b, h))] + cast_out,
        out_shape=[jax.ShapeDtypeStruct((batch * seq, A_HEADS * hv), BF16)] + cast_shapes,
        compiler_params=_params(("parallel", "parallel")),
        name="diff_attention",
    )(kvq, kvq, kvq, lam_vecs, g_subln_col, *[w for w, _ in cast_weights])


def kernel(x, positions, a_norm, m_w_in, m_b_igate, m_b_fgate, m_w_hnorm, m_w_out, kv_norm, w_kv, b_norm, w_q, lam_q1, lam_k1, lam_q2, lam_k2, subln, w_o, f_norm, w_up, conv_w, conv_b, w_down, final_norm):
    batch, seq, d = x.shape
    t = batch * seq
    depth = f_norm.shape[0]
    assert depth == 2 and a_norm.shape[0] == 1 and b_norm.shape[0] == 1
    assert seq % FFN_TM == 0 and seq % ATT_T == 0 and t % PROJ_TM == 0
    assert seq % MLSTM_L == 0 and 4 * (seq // MLSTM_L) <= MLSTM_L

    h = x.reshape(t, d)

    qkv_cols = 2 * M_HEADS * M_QK_DIM + M_HEADS * M_V_DIM
    o_cols = M_HEADS * M_V_DIM
    w_in = m_w_in[0].astype(BF16)
    w_gates = jnp.pad(m_w_in[0][:, qkv_cols + o_cols:], ((0, 0), (0, LANES - 2 * M_HEADS))).astype(BF16)
    qkv, o_gate, gates = _w_in_proj(h, a_norm[0][None, :], w_in, w_gates, qkv_cols, o_cols)

    nc = seq // MLSTM_L
    gates = gates[:, :2 * M_HEADS].reshape(batch, nc, MLSTM_L, 2, M_HEADS)
    gates_r = gates.transpose(0, 4, 3, 1, 2)
    bias = jnp.stack([m_b_igate[0], m_b_fgate[0]], axis=1)
    hg, w_up_0, w_down_0, w_out_b, w_q_b = _mlstm(
        qkv, o_gate, gates_r, bias[:, :, None], m_w_hnorm[0][:, None, :],
        ((w_up, 0), (w_down, 0), (m_w_out, 0), (w_q, 0)), batch, seq)
    h = _matmul_residual(hg, w_out_b, h, RES_TM, RES_TN, "w_out_residual")

    f_gain = f_norm[:, None, :]
    conv_b3 = conv_b[:, None, :]
    h = _conv_ffn(h, 0, f_gain, w_up_0, conv_w, conv_b3, w_down_0, final_norm[None, :], seq, False, "conv_ffn_0")

    layer = 1
    lambda_init = 0.8 - 0.6 * math.exp(-0.3 * layer)
    half = ROPE_DIM // 2
    inv_freq_col = (ROPE_THETA ** (-jnp.arange(half, dtype=F32) / half))[:, None]
    k_cols = A_HEADS * 2 * A_QK_DIM
    pos_rows = positions.reshape(t // PROJ_TM, 1, PROJ_TM)
    kvq = _kvq_proj(h, pos_rows, inv_freq_col, kv_norm[None, :], b_norm[0][None, :],
                    w_kv.astype(BF16), w_q_b, k_cols)
    lam_vecs = jnp.stack([lam_q1[0], lam_k1[0], lam_q2[0], lam_k2[0]], axis=0)
    att, w_up_1, w_down_1, w_o_b = _diff_attention(kvq, lam_vecs, subln[0][:, None],
                                                   ((w_up, 1), (w_down, 1), (w_o, 0)), batch, seq, lambda_init)
    h = _matmul_residual(att, w_o_b, h, RES_TM, RES_TN, "w_o_residual")

    h = _conv_ffn(h, 1, f_gain, w_up_1, conv_w, conv_b3, w_down_1, final_norm[None, :], seq, True, "conv_ffn_1")
    return h.reshape(batch, seq, d)
```

```python
import functools
import math

import jax
import jax.numpy as jnp
from jax import lax
from jax.experimental import pallas as pl
from jax.experimental.pallas import tpu as pltpu

F32 = jnp.float32
BF16 = jnp.bfloat16

EPS = 1e-6
M_HEADS = 8
M_QK_DIM = 128
M_V_DIM = 256
GATE_SOFTCAP = 15.0
A_HEADS = 8
A_QK_DIM = 128
A_V_DIM = 256
ROPE_DIM = 32
ROPE_THETA = 500000.0
CONV_W = 3

LANES = 128
V7X_VMEM_LIMIT_BYTES = 56 * 1024 * 1024

PROJ_TM = 1024
W_IN_TN = 1024
KVQ_TN = 1024
KVQ_ROWS = 512
RES_TM = 512
RES_TN = 512
FFN_TM = 1024
FFN_ROWS = 512
FFN_TF = 512
FFN_TAIL = 8
ATT_T = 256
MLSTM_L = LANES
MLSTM_HEADS = 2

NT_DIMS = (((1,), (1,)), ((), ()))
Q_SCALE_LOG2 = (A_QK_DIM ** -0.5) * math.log2(math.e)


def _rms_scale(x):
    return x * lax.rsqrt(jnp.mean(x * x, axis=-1, keepdims=True) + EPS)


def _params(semantics):
    return pltpu.CompilerParams(dimension_semantics=semantics, vmem_limit_bytes=V7X_VMEM_LIMIT_BYTES)


def _cast_specs(weights, steps, step_of):
    in_specs, out_specs, out_shapes = [], [], []
    for w, layer in weights:
        rows, cols = w.shape[1:]
        assert rows % steps == 0
        blk = rows // steps
        in_specs.append(pl.BlockSpec((None, blk, cols), lambda *idx, layer=layer: (layer, step_of(*idx), 0)))
        out_specs.append(pl.BlockSpec((blk, cols), lambda *idx: (step_of(*idx), 0)))
        out_shapes.append(jax.ShapeDtypeStruct((rows, cols), BF16))
    return in_specs, out_specs, out_shapes


def _cast_blocks(src_refs, dst_refs):
    for src, dst in zip(src_refs, dst_refs):
        dst[...] = src[...].astype(BF16)


def _tile_copy(x_hbm, x_buf, sem, tile):
    tm = x_buf.shape[0]
    return pltpu.make_async_copy(x_hbm.at[pl.ds(pl.multiple_of(tile * tm, tm), tm), :], x_buf, sem)


def _first_tile(x_hbm, x_buf, sem):
    @pl.when((pl.program_id(0) == 0) & (pl.program_id(1) == 0))
    def _():
        _tile_copy(x_hbm, x_buf, sem, 0).start()
        _tile_copy(x_hbm, x_buf, sem, 0).wait()


def _next_tile(i):
    return jnp.minimum(i + 1, pl.num_programs(0) - 1)


def _next_tile_wait(x_hbm, x_buf, sem):
    @pl.when(pl.program_id(1) == pl.num_programs(1) - 1)
    def _():
        _tile_copy(x_hbm, x_buf, sem, _next_tile(pl.program_id(0))).wait()


def _w_in_kernel(x_hbm, g_ref, w_ref, wg_ref, qkv_ref, o_ref, gates_ref, xn_ref, x_buf, sem, *, qkv_tiles):
    i = pl.program_id(0)
    j = pl.program_id(1)
    _first_tile(x_hbm, x_buf, sem)

    @pl.when(j == 0)
    def _():
        xn = (_rms_scale(x_buf[...]) * g_ref[...]).astype(BF16)
        xn_ref[...] = xn
        _tile_copy(x_hbm, x_buf, sem, _next_tile(i)).start()
        gates_ref[...] = jnp.dot(xn, wg_ref[...], preferred_element_type=F32)
        y = jnp.dot(xn, w_ref[...], preferred_element_type=F32)
        qkv_ref[...] = (y * (M_QK_DIM ** -0.5)).astype(BF16)

    @pl.when((j > 0) & (j < qkv_tiles))
    def _():
        qkv_ref[...] = jnp.dot(xn_ref[...], w_ref[...], preferred_element_type=F32).astype(BF16)

    @pl.when(j >= qkv_tiles)
    def _():
        o_ref[...] = jnp.dot(xn_ref[...], w_ref[...], preferred_element_type=F32)

    _next_tile_wait(x_hbm, x_buf, sem)


def _w_in_proj(x, g, w, w_gates, qkv_cols, o_cols):
    t, d = x.shape
    tm, tn = PROJ_TM, W_IN_TN
    assert M_HEADS * M_QK_DIM == tn and qkv_cols % tn == 0 and o_cols % tn == 0
    qkv_tiles = qkv_cols // tn
    return pl.pallas_call(
        functools.partial(_w_in_kernel, qkv_tiles=qkv_tiles),
        grid=(t // tm, (qkv_cols + o_cols) // tn),
        in_specs=[
            pl.BlockSpec(memory_space=pl.ANY),
            pl.BlockSpec((1, d), lambda i, j: (0, 0)),
            pl.BlockSpec((d, tn), lambda i, j: (0, j)),
            pl.BlockSpec((d, LANES), lambda i, j: (0, 0)),
        ],
        out_specs=[
            pl.BlockSpec((tm, tn), lambda i, j: (i, jnp.minimum(j, qkv_tiles - 1))),
            pl.BlockSpec((tm, tn), lambda i, j: (i, jnp.maximum(j - qkv_tiles, 0))),
            pl.BlockSpec((tm, LANES), lambda i, j: (i, 0)),
        ],
        out_shape=[
            jax.ShapeDtypeStruct((t, qkv_cols), BF16),
            jax.ShapeDtypeStruct((t, o_cols), F32),
            jax.ShapeDtypeStruct((t, LANES), F32),
        ],
        scratch_shapes=[pltpu.VMEM((tm, d), BF16), pltpu.VMEM((tm, d), F32), pltpu.SemaphoreType.DMA],
        compiler_params=_params(("arbitrary", "arbitrary")),
        name="norm_w_in",
    )(x, g, w, w_gates)


def _softcap(t):
    return GATE_SOFTCAP * jnp.tanh(t / GATE_SOFTCAP)


def _log_sigmoid(x):
    return jnp.minimum(x, 0.0) - jnp.log1p(jnp.exp(-jnp.abs(x)))


def _lane_scan(x, op, fill):
    lane = lax.broadcasted_iota(jnp.int32, x.shape, 1)
    d = 1
    while d < x.shape[1]:
        x = op(x, jnp.where(lane >= d, pltpu.roll(x, d, 1), fill))
        d *= 2
    return x


def _mlstm_gate_tables(gr, bias, nc):
    L = MLSTM_L
    i_rows = _softcap(gr[0] + bias[0:1, :])
    f_rows = _log_sigmoid(_softcap(gr[1] + bias[1:2, :]))
    b_rows = _lane_scan(f_rows, jnp.add, 0.0)
    a_rows = i_rows - b_rows
    pm_rows = _lane_scan(a_rows, jnp.maximum, -jnp.inf)
    b_last = b_rows[:, L - 1:L]
    a_max = pm_rows[:, L - 1:L]

    m = jnp.zeros((1, 1), F32)
    ms = [m]
    for c in range(nc):
        m = b_last[c:c + 1, :] + jnp.maximum(m, a_max[c:c + 1, :])
        ms.append(m)
    m_in = jnp.concatenate(ms[:nc], axis=0)
    m_out = jnp.concatenate(ms[1:], axis=0)
    big_m_rows = jnp.maximum(m_in, pm_rows)
    decay = jnp.exp(b_last + m_in - m_out)

    stacked = jnp.concatenate([
        big_m_rows,
        jnp.exp(m_in - big_m_rows),
        jnp.exp(-(b_rows + big_m_rows)),
        jnp.exp(a_rows + b_last - m_out),
        jnp.zeros((L - 4 * nc, L), F32)], axis=0)
    return a_rows, stacked.T, decay


def _mlstm_kernel(q_ref, k_ref, v_ref, o_ref, gr_ref, br_ref, wn_ref, *rest, seq, n_cast):
    cast_src, out_ref, cast_dst = rest[:n_cast], rest[n_cast], rest[n_cast + 1:]
    L = MLSTM_L
    nc = seq // L
    dk, dv = M_QK_DIM, M_V_DIM
    heads = range(MLSTM_HEADS)
    _cast_blocks(cast_src, cast_dst)
    tables = [_mlstm_gate_tables(gr_ref[hh], br_ref[hh], nc) for hh in heads]

    row_t = lax.broadcasted_iota(jnp.int32, (L, L), 0)
    col_s = lax.broadcasted_iota(jnp.int32, (L, L), 1)
    causal = row_t >= col_s
    ones_blk = jnp.ones((L, LANES), BF16)

    states = [jnp.zeros((dk, dv + LANES), F32) for _ in heads]
    for c in range(nc):
        rows = slice(c * L, (c + 1) * L)
        for hh in heads:
            a_rows, cols, decay = tables[hh]
            qb = q_ref[rows, hh * dk:(hh + 1) * dk]
            kb = k_ref[rows, hh * dk:(hh + 1) * dk]
            v_aug = jnp.concatenate([v_ref[rows, hh * dv:(hh + 1) * dv], ones_blk], axis=1)
            big_m = cols[:, c:c + 1]
            w_inter = cols[:, nc + c:nc + c + 1]
            e_neg_m = cols[:, 2 * nc + c:2 * nc + c + 1]
            w_key = cols[:, 3 * nc + c:3 * nc + c + 1]

            e = jnp.exp(jnp.where(causal, a_rows[c:c + 1, :] - big_m, -jnp.inf))
            s = lax.dot_general(qb, kb, NT_DIMS, preferred_element_type=F32) * e
            nd = (w_inter * jnp.dot(qb, states[hh].astype(BF16), preferred_element_type=F32)
                  + jnp.dot(s.astype(BF16), v_aug, preferred_element_type=F32))
            den = nd[:, dv:]
            r = 1.0 / jnp.maximum(jnp.abs(den), e_neg_m)
            h = nd[:, :dv] * jnp.concatenate([r] * (dv // LANES), axis=1)

            hn = _rms_scale(h) * wn_ref[hh]
            gate = jax.nn.sigmoid(o_ref[rows, hh * dv:(hh + 1) * dv])
            out_ref[rows, hh * dv:(hh + 1) * dv] = (hn * gate).astype(out_ref.dtype)

            kw_t = (kb.astype(F32) * w_key).T.astype(BF16)
            states[hh] = decay[c:c + 1, :] * states[hh] + jnp.dot(kw_t, v_aug, preferred_element_type=F32)


def _mlstm(qkv, o_gate, gates_r, bias_r, w_hnorm, cast_weights, batch, seq):
    nc = seq // MLSTM_L
    g = MLSTM_HEADS
    groups = M_HEADS // g
    cast_in, cast_out, cast_shapes = _cast_specs(cast_weights, batch * groups, lambda b, h: b * groups + h)
    qk_w, v_w = g * M_QK_DIM, g * M_V_DIM
    k_off = M_HEADS * M_QK_DIM // qk_w
    v_off = 2 * M_HEADS * M_QK_DIM // v_w
    return pl.pallas_call(
        functools.partial(_mlstm_kernel, seq=seq, n_cast=len(cast_weights)),
        grid=(batch, groups),
        in_specs=[
            pl.BlockSpec((seq, qk_w), lambda b, h: (b, h)),
            pl.BlockSpec((seq, qk_w), lambda b, h: (b, k_off + h)),
            pl.BlockSpec((seq, v_w), lambda b, h: (b, v_off + h)),
            pl.BlockSpec((seq, v_w), lambda b, h: (b, h)),
            pl.BlockSpec((None, g, 2, nc, MLSTM_L), lambda b, h: (b, h, 0, 0, 0)),
            pl.BlockSpec((g, 2, 1), lambda b, h: (h, 0, 0)),
            pl.BlockSpec((g, 1, M_V_DIM), lambda b, h: (h, 0, 0)),
        ] + cast_in,
        out_specs=[pl.BlockSpec((seq, v_w), lambda b, h: (b, h))] + cast_out,
        out_shape=[jax.ShapeDtypeStruct((batch * seq, M_HEADS * M_V_DIM), BF16)] + cast_shapes,
        compiler_params=_params(("parallel", "parallel")),
        name="mlstm_chunkwise",
    )(qkv, qkv, qkv, o_gate, gates_r, bias_r, w_hnorm, *[w for w, _ in cast_weights])


def _matmul_residual_kernel(a_ref, w_ref, r_ref, o_ref, *, tn):
    a = a_ref[...]
    for j in range(w_ref.shape[1] // tn):
        cols = slice(j * tn, (j + 1) * tn)
        o_ref[:, cols] = r_ref[:, cols] + jnp.dot(a, w_ref[:, cols], preferred_element_type=F32)


def _matmul_residual(a, w, res, tm, tn, name):
    t, kdim = a.shape
    n = w.shape[1]
    return pl.pallas_call(
        functools.partial(_matmul_residual_kernel, tn=tn),
        grid=(t // tm,),
        in_specs=[
            pl.BlockSpec((tm, kdim), lambda i: (i, 0)),
            pl.BlockSpec((kdim, n), lambda i: (0, 0)),
            pl.BlockSpec((tm, n), lambda i: (i, 0)),
        ],
        out_specs=pl.BlockSpec((tm, n), lambda i: (i, 0)),
        out_shape=jax.ShapeDtypeStruct((t, n), F32),
        compiler_params=_params(("parallel",)),
        name=name,
    )(a, w, res)


def _conv_ffn_kernel(h_ref, g_ref, wg_ref, wv_ref, cwg_ref, cwv_ref, cbg_ref, cbv_ref, wd_ref,
                     gf_ref, o_ref, xn_ref, tail_ref, u_ref, *, tiles_per_seq, final_norm):
    i = pl.program_id(0)
    f = pl.program_id(1)
    tm = h_ref.shape[0]

    @pl.when(i % tiles_per_seq == 0)
    def _():
        tail_ref[f] = jnp.zeros(tail_ref.shape[1:], F32)

    def step(first):
        if first:
            xn_ref[...] = (_rms_scale(h_ref[...]) * g_ref[...]).astype(BF16)
        history = [tail_ref[f, 0], tail_ref[f, 1]]
        for r in range(tm // FFN_ROWS):
            rows = slice(r * FFN_ROWS, (r + 1) * FFN_ROWS)
            xn = xn_ref[rows, :]

            def conv(w_ref, cw_ref, cb_ref, slot):
                u_ref[slot, :FFN_TAIL, :] = history[slot]
                u_ref[slot, FFN_TAIL:, :] = jnp.dot(xn, w_ref[...], preferred_element_type=F32)
                ue = u_ref[slot]
                history[slot] = ue[FFN_ROWS:, :]
                cw = cw_ref[...]
                c = (cb_ref[...] + pltpu.roll(ue, 2, 0) * cw[0:1, :] + pltpu.roll(ue, 1, 0) * cw[1:2, :]
                     + ue * cw[2:3, :])
                return c[FFN_TAIL:, :]

            gate = conv(wg_ref, cwg_ref, cbg_ref, 0)
            val = conv(wv_ref, cwv_ref, cbv_ref, 1)
            act = (gate * jax.nn.sigmoid(gate) * val).astype(BF16)
            base = h_ref[rows, :] if first else o_ref[rows, :]
            o_ref[rows, :] = base + jnp.dot(act, wd_ref[...], preferred_element_type=F32)
        tail_ref[f, 0] = history[0]
        tail_ref[f, 1] = history[1]

    @pl.when(f == 0)
    def _():
        step(True)

    @pl.when(f > 0)
    def _():
        step(False)

    if final_norm:
        @pl.when(f == pl.num_programs(1) - 1)
        def _():
            o_ref[...] = _rms_scale(o_ref[...]) * gf_ref[...]


def _conv_ffn(h, layer, g, w_up, conv_w, conv_b, w_down, g_final, seq, final_norm, name):
    t, d = h.shape
    d_ff = w_down.shape[0]
    tm, tf = FFN_TM, FFN_TF
    nf = d_ff // tf
    kern = functools.partial(_conv_ffn_kernel, tiles_per_seq=seq // tm, final_norm=final_norm)
    return pl.pallas_call(
        kern,
        grid=(t // tm, nf),
        in_specs=[
            pl.BlockSpec((tm, d), lambda i, f: (i, 0)),
            pl.BlockSpec((None, 1, d), lambda i, f: (layer, 0, 0)),
            pl.BlockSpec((d, tf), lambda i, f: (0, f)),
            pl.BlockSpec((d, tf), lambda i, f: (0, nf + f)),
            pl.BlockSpec((None, CONV_W, tf), lambda i, f: (layer, 0, f)),
            pl.BlockSpec((None, CONV_W, tf), lambda i, f: (layer, 0, nf + f)),
            pl.BlockSpec((None, 1, tf), lambda i, f: (layer, 0, f)),
            pl.BlockSpec((None, 1, tf), lambda i, f: (layer, 0, nf + f)),
            pl.BlockSpec((tf, d), lambda i, f: (f, 0)),
            pl.BlockSpec((1, d), lambda i, f: (0, 0)),
        ],
        out_specs=pl.BlockSpec((tm, d), lambda i, f: (i, 0)),
        out_shape=jax.ShapeDtypeStruct((t, d), F32),
        scratch_shapes=[
            pltpu.VMEM((tm, d), BF16),
            pltpu.VMEM((nf, 2, FFN_TAIL, tf), F32),
            pltpu.VMEM((2, FFN_TAIL + FFN_ROWS, tf), F32),
        ],
        compiler_params=_params(("arbitrary", "arbitrary")),
        name=name,
    )(h, g, w_up, w_up, conv_w, conv_w, conv_b, conv_b, w_down, g_final)


def _rope(y, cos_t, sin_lo, sin_hi):
    outs = []
    for gidx in range(y.shape[1] // LANES):
        blk = y[:, gidx * LANES:(gidx + 1) * LANES]
        half = ROPE_DIM // 2
        outs.append(blk * cos_t + pltpu.roll(blk, half, 1) * sin_hi + pltpu.roll(blk, LANES - half, 1) * sin_lo)
    return jnp.concatenate(outs, axis=1)


def _kvq_kernel(x_hbm, pos_ref, invf_ref, gkv_ref, gq_ref, wkv_ref, wq_ref, o_ref,
                xkv_ref, xq_ref, cos_ref, slo_ref, shi_ref, x_buf, sem, *, k_tiles, kv_tiles):
    i = pl.program_id(0)
    j = pl.program_id(1)
    _first_tile(x_hbm, x_buf, sem)

    def prologue():
        y = _rms_scale(x_buf[...])
        xkv_ref[...] = (y * gkv_ref[...]).astype(BF16)
        xq_ref[...] = (y * gq_ref[...]).astype(BF16)
        _tile_copy(x_hbm, x_buf, sem, _next_tile(i)).start()
        half = ROPE_DIM // 2
        ang = invf_ref[...] * pos_ref[...].astype(F32)
        cos = jnp.cos(ang)
        sin = jnp.sin(ang)
        pad = LANES - ROPE_DIM
        tm = ang.shape[1]
        cos_ref[...] = jnp.concatenate([cos, cos, jnp.ones((pad, tm), F32)], axis=0).T
        sin_t = jnp.concatenate([-sin, sin, jnp.zeros((pad, tm), F32)], axis=0).T
        lane = lax.broadcasted_iota(jnp.int32, sin_t.shape, 1)
        slo_ref[...] = jnp.where(lane < half, sin_t, 0.0)
        shi_ref[...] = jnp.where(lane >= half, sin_t, 0.0)

    halves = [slice(r * KVQ_ROWS, (r + 1) * KVQ_ROWS) for r in range(x_buf.shape[0] // KVQ_ROWS)]

    def rotated(xn_ref, w_ref, rows, scale):
        y = jnp.dot(xn_ref[rows, :], w_ref[...], preferred_element_type=F32)
        y = _rope(y, cos_ref[rows, :], slo_ref[rows, :], shi_ref[rows, :])
        return y if scale is None else y * scale

    def k_step():
        for rows in halves:
            o_ref[rows, :] = rotated(xkv_ref, wkv_ref, rows, None).astype(BF16)

    @pl.when(j == 0)
    def _():
        prologue()
        k_step()

    @pl.when((j > 0) & (j < k_tiles))
    def _():
        k_step()

    @pl.when((j >= k_tiles) & (j < kv_tiles))
    def _():
        o_ref[...] = jnp.dot(xkv_ref[...], wkv_ref[...], preferred_element_type=F32).astype(BF16)

    @pl.when(j >= kv_tiles)
    def _():
        for rows in halves:
            o_ref[rows, :] = rotated(xq_ref, wq_ref, rows, Q_SCALE_LOG2).astype(BF16)

    _next_tile_wait(x_hbm, x_buf, sem)


def _kvq_proj(h, pos_rows, inv_freq_col, g_kv, g_q, w_kv, w_q, k_cols):
    t, d = h.shape
    tm, tn = PROJ_TM, KVQ_TN
    k_tiles = k_cols // tn
    kv_tiles = w_kv.shape[1] // tn
    q_tiles = w_q.shape[1] // tn
    kern = functools.partial(_kvq_kernel, k_tiles=k_tiles, kv_tiles=kv_tiles)
    return pl.pallas_call(
        kern,
        grid=(t // tm, kv_tiles + q_tiles),
        in_specs=[
            pl.BlockSpec(memory_space=pl.ANY),
            pl.BlockSpec((None, 1, tm), lambda i, j: (i, 0, 0)),
            pl.BlockSpec((ROPE_DIM // 2, 1), lambda i, j: (0, 0)),
            pl.BlockSpec((1, d), lambda i, j: (0, 0)),
            pl.BlockSpec((1, d), lambda i, j: (0, 0)),
            pl.BlockSpec((d, tn), lambda i, j: (0, jnp.minimum(j, kv_tiles - 1))),
            pl.BlockSpec((d, tn), lambda i, j: (0, jnp.maximum(j - kv_tiles, 0))),
        ],
        out_specs=pl.BlockSpec((tm, tn), lambda i, j: (i, j)),
        out_shape=jax.ShapeDtypeStruct((t, (kv_tiles + q_tiles) * tn), BF16),
        scratch_shapes=[
            pltpu.VMEM((tm, d), BF16),
            pltpu.VMEM((tm, d), BF16),
            pltpu.VMEM((tm, LANES), F32),
            pltpu.VMEM((tm, LANES), F32),
            pltpu.VMEM((tm, LANES), F32),
            pltpu.VMEM((tm, d), F32),
            pltpu.SemaphoreType.DMA,
        ],
        compiler_params=_params(("arbitrary", "arbitrary")),
        name="norm_kvq_rope",
    )(h, pos_rows, inv_freq_col, g_kv, g_q, w_kv, w_q)


def _diff_attn_kernel(k_ref, v_ref, q_ref, lam_ref, g_ref, *rest, lambda_init, seq, n_cast):
    cast_src, o_ref, cast_dst = rest[:n_cast], rest[n_cast], rest[n_cast + 1:]
    tb = ATT_T
    _cast_blocks(cast_src, cast_dst)
    dk = A_QK_DIM
    nblk = seq // tb

    lv = lam_ref[...]
    lam = (jnp.exp(jnp.sum(lv[0:1, :] * lv[1:2, :], axis=1, keepdims=True))
           - jnp.exp(jnp.sum(lv[2:3, :] * lv[3:4, :], axis=1, keepdims=True)) + lambda_init)
    g_col = g_ref[...] * (1.0 - lambda_init)
    key_idx = lax.broadcasted_iota(jnp.int32, (tb, tb), 0)
    qry_idx = lax.broadcasted_iota(jnp.int32, (tb, tb), 1)
    diag_mask = key_idx <= qry_idx
    v_t = [v_ref[kb * tb:(kb + 1) * tb, :].astype(F32).T.astype(BF16) for kb in range(nblk)]

    for qi in range(nblk):
        q = q_ref[qi * tb:(qi + 1) * tb, :]
        comps = []
        for c in range(2):
            qc = q[:, c * dk:(c + 1) * dk]
            m = l = acc = None
            for kb in range(qi + 1):
                kc = k_ref[kb * tb:(kb + 1) * tb, c * dk:(c + 1) * dk]
                s = lax.dot_general(kc, qc, NT_DIMS, preferred_element_type=F32)
                if kb == qi:
                    s = jnp.where(diag_mask, s, -jnp.inf)
                s_max = jnp.max(s, axis=0, keepdims=True)
                if kb == 0:
                    m = s_max
                    p = jnp.exp2(s - m)
                    l = jnp.sum(p, axis=0, keepdims=True)
                    acc = jnp.dot(v_t[kb], p.astype(BF16), preferred_element_type=F32)
                else:
                    m_new = jnp.maximum(m, s_max)
                    alpha = jnp.exp2(m - m_new)
                    p = jnp.exp2(s - m_new)
                    l = alpha * l + jnp.sum(p, axis=0, keepdims=True)
                    acc = alpha * acc + jnp.dot(v_t[kb], p.astype(BF16), preferred_element_type=F32)
                    m = m_new
            comps.append(acc * (1.0 / l))
        o_t = comps[0] - lam * comps[1]
        o_t = o_t * lax.rsqrt(jnp.mean(o_t * o_t, axis=0, keepdims=True) + EPS) * g_col
        o_ref[qi * tb:(qi + 1) * tb, :] = o_t.T.astype(o_ref.dtype)


def _diff_attention(kvq, lam_vecs, g_subln_col, cast_weights, batch, seq, lambda_init):
    hv = A_V_DIM
    cast_in, cast_out, cast_shapes = _cast_specs(cast_weights, batch * A_HEADS, lambda b, h: b * A_HEADS + h)
    return pl.pallas_call(
        functools.partial(_diff_attn_kernel, lambda_init=lambda_init, seq=seq, n_cast=len(cast_weights)),
        grid=(batch, A_HEADS),
        in_specs=[
            pl.BlockSpec((seq, hv), lambda b, h: (b, h)),
            pl.BlockSpec((seq, hv), lambda b, h: (b, A_HEADS + h)),
            pl.BlockSpec((seq, hv), lambda b, h: (b, 2 * A_HEADS + h)),
            pl.BlockSpec((4, A_QK_DIM), lambda b, h: (0, 0)),
            pl.BlockSpec((hv, 1), lambda b, h: (0, 0)),
        ] + cast_in,
        out_specs=[pl.BlockSpec((seq, hv), lambda b, h: (b, h))] + cast_out,
        out_shape=[jax.ShapeDtypeStruct((batch * seq, A_HEADS * hv), BF16)] + cast_shapes,
        compiler_params=_params(("parallel", "parallel")),
        name="diff_attention",
    )(kvq, kvq, kvq, lam_vecs, g_subln_col, *[w for w, _ in cast_weights])


def kernel(x, positions, a_norm, m_w_in, m_b_igate, m_b_fgate, m_w_hnorm, m_w_out, kv_norm, w_kv, b_norm, w_q, lam_q1, lam_k1, lam_q2, lam_k2, subln, w_o, f_norm, w_up, conv_w, conv_b, w_down, final_norm):
    batch, seq, d = x.shape
    t = batch * seq
    depth = f_norm.shape[0]
    assert depth == 2 and a_norm.shape[0] == 1 and b_norm.shape[0] == 1
    assert seq % FFN_TM == 0 and seq % ATT_T == 0 and t % PROJ_TM == 0
    assert seq % MLSTM_L == 0 and 4 * (seq // MLSTM_L) <= MLSTM_L

    h = x.reshape(t, d)

    qkv_cols = 2 * M_HEADS * M_QK_DIM + M_HEADS * M_V_DIM
    o_cols = M_HEADS * M_V_DIM
    w_in = m_w_in[0].astype(BF16)
    w_gates = jnp.pad(m_w_in[0][:, qkv_cols + o_cols:], ((0, 0), (0, LANES - 2 * M_HEADS))).astype(BF16)
    qkv, o_gate, gates = _w_in_proj(h, a_norm[0][None, :], w_in, w_gates, qkv_cols, o_cols)

    nc = seq // MLSTM_L
    gates = gates[:, :2 * M_HEADS].reshape(batch, nc, MLSTM_L, 2, M_HEADS)
    gates_r = gates.transpose(0, 4, 3, 1, 2)
    bias = jnp.stack([m_b_igate[0], m_b_fgate[0]], axis=1)
    hg, w_up_0, w_down_0, w_out_b, w_q_b = _mlstm(
        qkv, o_gate, gates_r, bias[:, :, None], m_w_hnorm[0][:, None, :],
        ((w_up, 0), (w_down, 0), (m_w_out, 0), (w_q, 0)), batch, seq)
    h = _matmul_residual(hg, w_out_b, h, RES_TM, RES_TN, "w_out_residual")

    f_gain = f_norm[:, None, :]
    conv_b3 = conv_b[:, None, :]
    h = _conv_ffn(h, 0, f_gain, w_up_0, conv_w, conv_b3, w_down_0, final_norm[None, :], seq, False, "conv_ffn_0")

    layer = 1
    lambda_init = 0.8 - 0.6 * math.exp(-0.3 * layer)
    half = ROPE_DIM // 2
    inv_freq_col = (ROPE_THETA ** (-jnp.arange(half, dtype=F32) / half))[:, None]
    k_cols = A_HEADS * 2 * A_QK_DIM
    pos_rows = positions.reshape(t // PROJ_TM, 1, PROJ_TM)
    kvq = _kvq_proj(h, pos_rows, inv_freq_col, kv_norm[None, :], b_norm[0][None, :],
                    w_kv.astype(BF16), w_q_b, k_cols)
    lam_vecs = jnp.stack([lam_q1[0], lam_k1[0], lam_q2[0], lam_k2[0]], axis=0)
    att, w_up_1, w_down_1, w_o_b = _diff_attention(kvq, lam_vecs, subln[0][:, None],
                                                   ((w_up, 1), (w_down, 1), (w_o, 0)), batch, seq, lambda_init)
    h = _matmul_residual(att, w_o_b, h, RES_TM, RES_TN, "w_o_residual")

    h = _conv_ffn(h, 1, f_gain, w_up_1, conv_w, conv_b3, w_down_1, final_norm[None, :], seq, True, "conv_ffn_1")
    return h.reshape(batch, seq, d)
```

```python
import functools
import math

import jax
import jax.numpy as jnp
from jax import lax
from jax.experimental import pallas as pl
from jax.experimental.pallas import tpu as pltpu

F32 = jnp.float32
BF16 = jnp.bfloat16

EPS = 1e-6
M_HEADS = 8
M_QK_DIM = 128
M_V_DIM = 256
GATE_SOFTCAP = 15.0
A_HEADS = 8
A_QK_DIM = 128
A_V_DIM = 256
ROPE_DIM = 32
ROPE_THETA = 500000.0
CONV_W = 3

LANES = 128
V7X_VMEM_LIMIT_BYTES = 56 * 1024 * 1024

PROJ_TM = 1024
W_IN_TN = 1024
KVQ_TN = 1024
KVQ_ROWS = 512
RES_TM = 512
RES_TN = 512
FFN_TM = 1024
FFN_ROWS = 512
FFN_TF = 512
FFN_TAIL = 8
ATT_T = 256
MLSTM_L = LANES
MLSTM_HEADS = 2

NT_DIMS = (((1,), (1,)), ((), ()))
Q_SCALE_LOG2 = (A_QK_DIM ** -0.5) * math.log2(math.e)


def _rms_scale(x):
    return x * lax.rsqrt(jnp.mean(x * x, axis=-1, keepdims=True) + EPS)


def _params(semantics):
    return pltpu.CompilerParams(dimension_semantics=semantics, vmem_limit_bytes=V7X_VMEM_LIMIT_BYTES)


def _cast_specs(weights, steps, step_of):
    in_specs, out_specs, out_shapes = [], [], []
    for w, layer in weights:
        rows, cols = w.shape[1:]
        assert rows % steps == 0
        blk = rows // steps
        in_specs.append(pl.BlockSpec((None, blk, cols), lambda *idx, layer=layer: (layer, step_of(*idx), 0)))
        out_specs.append(pl.BlockSpec((blk, cols), lambda *idx: (step_of(*idx), 0)))
        out_shapes.append(jax.ShapeDtypeStruct((rows, cols), BF16))
    return in_specs, out_specs, out_shapes


def _cast_blocks(src_refs, dst_refs):
    for src, dst in zip(src_refs, dst_refs):
        dst[...] = src[...].astype(BF16)


def _tile_copy(x_hbm, x_buf, sem, tile):
    tm = x_buf.shape[0]
    return pltpu.make_async_copy(x_hbm.at[pl.ds(pl.multiple_of(tile * tm, tm), tm), :], x_buf, sem)


def _first_tile(x_hbm, x_buf, sem):
    @pl.when((pl.program_id(0) == 0) & (pl.program_id(1) == 0))
    def _():
        _tile_copy(x_hbm, x_buf, sem, 0).start()
        _tile_copy(x_hbm, x_buf, sem, 0).wait()


def _next_tile(i):
    return jnp.minimum(i + 1, pl.num_programs(0) - 1)


def _next_tile_wait(x_hbm, x_buf, sem):
    @pl.when(pl.program_id(1) == pl.num_programs(1) - 1)
    def _():
        _tile_copy(x_hbm, x_buf, sem, _next_tile(pl.program_id(0))).wait()


def _w_in_kernel(x_hbm, g_ref, w_ref, wg_ref, *rest, qkv_tiles, n_cast):
    cast_src, (qkv_ref, o_ref, gates_ref) = rest[:n_cast], rest[n_cast:n_cast + 3]
    cast_dst, (xn_ref, x_buf, sem) = rest[n_cast + 3:2 * n_cast + 3], rest[2 * n_cast + 3:]
    i = pl.program_id(0)
    j = pl.program_id(1)
    _first_tile(x_hbm, x_buf, sem)

    @pl.when(j == 0)
    def _():
        xn = (_rms_scale(x_buf[...]) * g_ref[...]).astype(BF16)
        xn_ref[...] = xn
        _tile_copy(x_hbm, x_buf, sem, _next_tile(i)).start()
        gates_ref[...] = jnp.dot(xn, wg_ref[...], preferred_element_type=F32)
        y = jnp.dot(xn, w_ref[...], preferred_element_type=F32)
        qkv_ref[...] = (y * (M_QK_DIM ** -0.5)).astype(BF16)
        _cast_blocks(cast_src, cast_dst)

    @pl.when((j > 0) & (j < qkv_tiles))
    def _():
        qkv_ref[...] = jnp.dot(xn_ref[...], w_ref[...], preferred_element_type=F32).astype(BF16)
        _cast_blocks(cast_src, cast_dst)

    @pl.when(j >= qkv_tiles)
    def _():
        o_ref[...] = jnp.dot(xn_ref[...], w_ref[...], preferred_element_type=F32)
        _cast_blocks(cast_src, cast_dst)

    _next_tile_wait(x_hbm, x_buf, sem)


def _w_in_proj(x, g, w, w_gates, qkv_cols, o_cols, cast_weights):
    t, d = x.shape
    tm, tn = PROJ_TM, W_IN_TN
    assert M_HEADS * M_QK_DIM == tn and qkv_cols % tn == 0 and o_cols % tn == 0
    qkv_tiles = qkv_cols // tn
    n_j = (qkv_cols + o_cols) // tn
    cast_in, cast_out, cast_shapes = [], [], []
    for w_cast, layer, steps in cast_weights:
        assert steps <= (t // tm) * n_j
        specs = _cast_specs(((w_cast, layer),), steps, lambda i, j, steps=steps: jnp.minimum(i * n_j + j, steps - 1))
        cast_in, cast_out, cast_shapes = cast_in + specs[0], cast_out + specs[1], cast_shapes + specs[2]
    return pl.pallas_call(
        functools.partial(_w_in_kernel, qkv_tiles=qkv_tiles, n_cast=len(cast_weights)),
        grid=(t // tm, n_j),
        in_specs=[
            pl.BlockSpec(memory_space=pl.ANY),
            pl.BlockSpec((1, d), lambda i, j: (0, 0)),
            pl.BlockSpec((d, tn), lambda i, j: (0, j)),
            pl.BlockSpec((d, LANES), lambda i, j: (0, 0)),
        ] + cast_in,
        out_specs=[
            pl.BlockSpec((tm, tn), lambda i, j: (i, jnp.minimum(j, qkv_tiles - 1))),
            pl.BlockSpec((tm, tn), lambda i, j: (i, jnp.maximum(j - qkv_tiles, 0))),
            pl.BlockSpec((tm, LANES), lambda i, j: (i, 0)),
        ] + cast_out,
        out_shape=[
            jax.ShapeDtypeStruct((t, qkv_cols), BF16),
            jax.ShapeDtypeStruct((t, o_cols), F32),
            jax.ShapeDtypeStruct((t, LANES), F32),
        ] + cast_shapes,
        scratch_shapes=[pltpu.VMEM((tm, d), BF16), pltpu.VMEM((tm, d), F32), pltpu.SemaphoreType.DMA],
        compiler_params=_params(("arbitrary", "arbitrary")),
        name="norm_w_in",
    )(x, g, w, w_gates, *[w_cast for w_cast, _, _ in cast_weights])


def _softcap(t):
    return GATE_SOFTCAP * jnp.tanh(t / GATE_SOFTCAP)


def _log_sigmoid(x):
    return jnp.minimum(x, 0.0) - jnp.log1p(jnp.exp(-jnp.abs(x)))


def _lane_scan(x, op, fill):
    lane = lax.broadcasted_iota(jnp.int32, x.shape, 1)
    d = 1
    while d < x.shape[1]:
        x = op(x, jnp.where(lane >= d, pltpu.roll(x, d, 1), fill))
        d *= 2
    return x


def _mlstm_gate_tables(gr, bias, nc):
    L = MLSTM_L
    i_rows = _softcap(gr[0] + bias[0:1, :])
    f_rows = _log_sigmoid(_softcap(gr[1] + bias[1:2, :]))
    b_rows = _lane_scan(f_rows, jnp.add, 0.0)
    a_rows = i_rows - b_rows
    pm_rows = _lane_scan(a_rows, jnp.maximum, -jnp.inf)
    b_last = b_rows[:, L - 1:L]
    a_max = pm_rows[:, L - 1:L]

    m = jnp.zeros((1, 1), F32)
    ms = [m]
    for c in range(nc):
        m = b_last[c:c + 1, :] + jnp.maximum(m, a_max[c:c + 1, :])
        ms.append(m)
    m_in = jnp.concatenate(ms[:nc], axis=0)
    m_out = jnp.concatenate(ms[1:], axis=0)
    big_m_rows = jnp.maximum(m_in, pm_rows)
    decay = jnp.exp(b_last + m_in - m_out)

    stacked = jnp.concatenate([
        big_m_rows,
        jnp.exp(m_in - big_m_rows),
        jnp.exp(-(b_rows + big_m_rows)),
        jnp.exp(a_rows + b_last - m_out),
        jnp.zeros((L - 4 * nc, L), F32)], axis=0)
    return a_rows, stacked.T, decay


def _mlstm_kernel(q_ref, k_ref, v_ref, o_ref, gr_ref, br_ref, wn_ref, *rest, seq, n_cast):
    cast_src, out_ref, cast_dst = rest[:n_cast], rest[n_cast], rest[n_cast + 1:]
    L = MLSTM_L
    nc = seq // L
    dk, dv = M_QK_DIM, M_V_DIM
    heads = range(MLSTM_HEADS)
    _cast_blocks(cast_src, cast_dst)
    tables = [_mlstm_gate_tables(gr_ref[hh], br_ref[hh], nc) for hh in heads]

    row_t = lax.broadcasted_iota(jnp.int32, (L, L), 0)
    col_s = lax.broadcasted_iota(jnp.int32, (L, L), 1)
    causal = row_t >= col_s
    ones_blk = jnp.ones((L, LANES), BF16)

    states = [jnp.zeros((dk, dv + LANES), F32) for _ in heads]
    for c in range(nc):
        rows = slice(c * L, (c + 1) * L)
        for hh in heads:
            a_rows, cols, decay = tables[hh]
            qb = q_ref[rows, hh * dk:(hh + 1) * dk]
            kb = k_ref[rows, hh * dk:(hh + 1) * dk]
            v_aug = jnp.concatenate([v_ref[rows, hh * dv:(hh + 1) * dv], ones_blk], axis=1)
            big_m = cols[:, c:c + 1]
            w_inter = cols[:, nc + c:nc + c + 1]
            e_neg_m = cols[:, 2 * nc + c:2 * nc + c + 1]
            w_key = cols[:, 3 * nc + c:3 * nc + c + 1]

            e = jnp.exp(jnp.where(causal, a_rows[c:c + 1, :] - big_m, -jnp.inf))
            s = lax.dot_general(qb, kb, NT_DIMS, preferred_element_type=F32) * e
            nd = (w_inter * jnp.dot(qb, states[hh].astype(BF16), preferred_element_type=F32)
                  + jnp.dot(s.astype(BF16), v_aug, preferred_element_type=F32))
            den = nd[:, dv:]
            r = 1.0 / jnp.maximum(jnp.abs(den), e_neg_m)
            h = nd[:, :dv] * jnp.concatenate([r] * (dv // LANES), axis=1)

            hn = _rms_scale(h) * wn_ref[hh]
            gate = jax.nn.sigmoid(o_ref[rows, hh * dv:(hh + 1) * dv])
            out_ref[rows, hh * dv:(hh + 1) * dv] = (hn * gate).astype(out_ref.dtype)

            kw_t = (kb.astype(F32) * w_key).T.astype(BF16)
            states[hh] = decay[c:c + 1, :] * states[hh] + jnp.dot(kw_t, v_aug, preferred_element_type=F32)


def _mlstm(qkv, o_gate, gates_r, bias_r, w_hnorm, cast_weights, batch, seq):
    nc = seq // MLSTM_L
    g = MLSTM_HEADS
    groups = M_HEADS // g
    cast_in, cast_out, cast_shapes = _cast_specs(cast_weights, batch * groups, lambda b, h: b * groups + h)
    qk_w, v_w = g * M_QK_DIM, g * M_V_DIM
    k_off = M_HEADS * M_QK_DIM // qk_w
    v_off = 2 * M_HEADS * M_QK_DIM // v_w
    return pl.pallas_call(
        functools.partial(_mlstm_kernel, seq=seq, n_cast=len(cast_weights)),
        grid=(batch, groups),
        in_specs=[
            pl.BlockSpec((seq, qk_w), lambda b, h: (b, h)),
            pl.BlockSpec((seq, qk_w), lambda b, h: (b, k_off + h)),
            pl.BlockSpec((seq, v_w), lambda b, h: (b, v_off + h)),
            pl.BlockSpec((seq, v_w), lambda b, h: (b, h)),
            pl.BlockSpec((None, g, 2, nc, MLSTM_L), lambda b, h: (b, h, 0, 0, 0)),
            pl.BlockSpec((g, 2, 1), lambda b, h: (h, 0, 0)),
            pl.BlockSpec((g, 1, M_V_DIM), lambda b, h: (h, 0, 0)),
        ] + cast_in,
        out_specs=[pl.BlockSpec((seq, v_w), lambda b, h: (b, h))] + cast_out,
        out_shape=[jax.ShapeDtypeStruct((batch * seq, M_HEADS * M_V_DIM), BF16)] + cast_shapes,
        compiler_params=_params(("parallel", "parallel")),
        name="mlstm_chunkwise",
    )(qkv, qkv, qkv, o_gate, gates_r, bias_r, w_hnorm, *[w for w, _ in cast_weights])


def _matmul_residual_kernel(a_ref, w_ref, r_ref, o_ref, *, tn):
    a = a_ref[...]
    for j in range(w_ref.shape[1] // tn):
        cols = slice(j * tn, (j + 1) * tn)
        o_ref[:, cols] = r_ref[:, cols] + jnp.dot(a, w_ref[:, cols], preferred_element_type=F32)


def _matmul_residual(a, w, res, tm, tn, name):
    t, kdim = a.shape
    n = w.shape[1]
    return pl.pallas_call(
        functools.partial(_matmul_residual_kernel, tn=tn),
        grid=(t // tm,),
        in_specs=[
            pl.BlockSpec((tm, kdim), lambda i: (i, 0)),
            pl.BlockSpec((kdim, n), lambda i: (0, 0)),
            pl.BlockSpec((tm, n), lambda i: (i, 0)),
        ],
        out_specs=pl.BlockSpec((tm, n), lambda i: (i, 0)),
        out_shape=jax.ShapeDtypeStruct((t, n), F32),
        compiler_params=_params(("parallel",)),
        name=name,
    )(a, w, res)


def _conv_ffn_kernel(h_ref, g_ref, wg_ref, wv_ref, cwg_ref, cwv_ref, cbg_ref, cbv_ref, wd_ref,
                     gf_ref, o_ref, xn_ref, tail_ref, u_ref, *, tiles_per_seq, final_norm):
    i = pl.program_id(0)
    f = pl.program_id(1)
    tm = h_ref.shape[0]

    @pl.when(i % tiles_per_seq == 0)
    def _():
        tail_ref[f] = jnp.zeros(tail_ref.shape[1:], F32)

    def step(first):
        if first:
            xn_ref[...] = (_rms_scale(h_ref[...]) * g_ref[...]).astype(BF16)
        history = [tail_ref[f, 0], tail_ref[f, 1]]
        for r in range(tm // FFN_ROWS):
            rows = slice(r * FFN_ROWS, (r + 1) * FFN_ROWS)
            xn = xn_ref[rows, :]

            def conv(w_ref, cw_ref, cb_ref, slot):
                u_ref[slot, :FFN_TAIL, :] = history[slot]
                u_ref[slot, FFN_TAIL:, :] = jnp.dot(xn, w_ref[...], preferred_element_type=F32)
                ue = u_ref[slot]
                history[slot] = ue[FFN_ROWS:, :]
                cw = cw_ref[...]
                c = (cb_ref[...] + pltpu.roll(ue, 2, 0) * cw[0:1, :] + pltpu.roll(ue, 1, 0) * cw[1:2, :]
                     + ue * cw[2:3, :])
                return c[FFN_TAIL:, :]

            gate = conv(wg_ref, cwg_ref, cbg_ref, 0)
            val = conv(wv_ref, cwv_ref, cbv_ref, 1)
            act = (gate * jax.nn.sigmoid(gate) * val).astype(BF16)
            base = h_ref[rows, :] if first else o_ref[rows, :]
            o_ref[rows, :] = base + jnp.dot(act, wd_ref[...], preferred_element_type=F32)
        tail_ref[f, 0] = history[0]
        tail_ref[f, 1] = history[1]

    @pl.when(f == 0)
    def _():
        step(True)

    @pl.when(f > 0)
    def _():
        step(False)

    if final_norm:
        @pl.when(f == pl.num_programs(1) - 1)
        def _():
            o_ref[...] = _rms_scale(o_ref[...]) * gf_ref[...]


def _conv_ffn(h, layer, g, w_up, conv_w, conv_b, w_down, g_final, seq, final_norm, name):
    t, d = h.shape
    d_ff = w_down.shape[0]
    tm, tf = FFN_TM, FFN_TF
    nf = d_ff // tf
    kern = functools.partial(_conv_ffn_kernel, tiles_per_seq=seq // tm, final_norm=final_norm)
    return pl.pallas_call(
        kern,
        grid=(t // tm, nf),
        in_specs=[
            pl.BlockSpec((tm, d), lambda i, f: (i, 0)),
            pl.BlockSpec((None, 1, d), lambda i, f: (layer, 0, 0)),
            pl.BlockSpec((d, tf), lambda i, f: (0, f)),
            pl.BlockSpec((d, tf), lambda i, f: (0, nf + f)),
            pl.BlockSpec((None, CONV_W, tf), lambda i, f: (layer, 0, f)),
            pl.BlockSpec((None, CONV_W, tf), lambda i, f: (layer, 0, nf + f)),
            pl.BlockSpec((None, 1, tf), lambda i, f: (layer, 0, f)),
            pl.BlockSpec((None, 1, tf), lambda i, f: (layer, 0, nf + f)),
            pl.BlockSpec((tf, d), lambda i, f: (f, 0)),
            pl.BlockSpec((1, d), lambda i, f: (0, 0)),
        ],
        out_specs=pl.BlockSpec((tm, d), lambda i, f: (i, 0)),
        out_shape=jax.ShapeDtypeStruct((t, d), F32),
        scratch_shapes=[
            pltpu.VMEM((tm, d), BF16),
            pltpu.VMEM((nf, 2, FFN_TAIL, tf), F32),
            pltpu.VMEM((2, FFN_TAIL + FFN_ROWS, tf), F32),
        ],
        compiler_params=_params(("arbitrary", "arbitrary")),
        name=name,
    )(h, g, w_up, w_up, conv_w, conv_w, conv_b, conv_b, w_down, g_final)


def _rope(y, cos_t, sin_lo, sin_hi):
    outs = []
    for gidx in range(y.shape[1] // LANES):
        blk = y[:, gidx * LANES:(gidx + 1) * LANES]
        half = ROPE_DIM // 2
        outs.append(blk * cos_t + pltpu.roll(blk, half, 1) * sin_hi + pltpu.roll(blk, LANES - half, 1) * sin_lo)
    return jnp.concatenate(outs, axis=1)


def _kvq_kernel(x_hbm, pos_ref, invf_ref, gkv_ref, gq_ref, wkv_ref, wq_ref, o_ref,
                xkv_ref, xq_ref, cos_ref, slo_ref, shi_ref, x_buf, sem, *, k_tiles, kv_tiles):
    i = pl.program_id(0)
    j = pl.program_id(1)
    _first_tile(x_hbm, x_buf, sem)

    def prologue():
        y = _rms_scale(x_buf[...])
        xkv_ref[...] = (y * gkv_ref[...]).astype(BF16)
        xq_ref[...] = (y * gq_ref[...]).astype(BF16)
        _tile_copy(x_hbm, x_buf, sem, _next_tile(i)).start()
        half = ROPE_DIM // 2
        ang = invf_ref[...] * pos_ref[...].astype(F32)
        cos = jnp.cos(ang)
        sin = jnp.sin(ang)
        pad = LANES - ROPE_DIM
        tm = ang.shape[1]
        cos_ref[...] = jnp.concatenate([cos, cos, jnp.ones((pad, tm), F32)], axis=0).T
        sin_t = jnp.concatenate([-sin, sin, jnp.zeros((pad, tm), F32)], axis=0).T
        lane = lax.broadcasted_iota(jnp.int32, sin_t.shape, 1)
        slo_ref[...] = jnp.where(lane < half, sin_t, 0.0)
        shi_ref[...] = jnp.where(lane >= half, sin_t, 0.0)

    halves = [slice(r * KVQ_ROWS, (r + 1) * KVQ_ROWS) for r in range(x_buf.shape[0] // KVQ_ROWS)]

    def rotated(xn_ref, w_ref, rows, scale):
        y = jnp.dot(xn_ref[rows, :], w_ref[...], preferred_element_type=F32)
        y = _rope(y, cos_ref[rows, :], slo_ref[rows, :], shi_ref[rows, :])
        return y if scale is None else y * scale

    def k_step():
        for rows in halves:
            o_ref[rows, :] = rotated(xkv_ref, wkv_ref, rows, None).astype(BF16)

    @pl.when(j == 0)
    def _():
        prologue()
        k_step()

    @pl.when((j > 0) & (j < k_tiles))
    def _():
        k_step()

    @pl.when((j >= k_tiles) & (j < kv_tiles))
    def _():
        o_ref[...] = jnp.dot(xkv_ref[...], wkv_ref[...], preferred_element_type=F32).astype(BF16)

    @pl.when(j >= kv_tiles)
    def _():
        for rows in halves:
            o_ref[rows, :] = rotated(xq_ref, wq_ref, rows, Q_SCALE_LOG2).astype(BF16)

    _next_tile_wait(x_hbm, x_buf, sem)


def _kvq_proj(h, pos_rows, inv_freq_col, g_kv, g_q, w_kv, w_q, k_cols):
    t, d = h.shape
    tm, tn = PROJ_TM, KVQ_TN
    k_tiles = k_cols // tn
    kv_tiles = w_kv.shape[1] // tn
    q_tiles = w_q.shape[1] // tn
    kern = functools.partial(_kvq_kernel, k_tiles=k_tiles, kv_tiles=kv_tiles)
    return pl.pallas_call(
        kern,
        grid=(t // tm, kv_tiles + q_tiles),
        in_specs=[
            pl.BlockSpec(memory_space=pl.ANY),
            pl.BlockSpec((None, 1, tm), lambda i, j: (i, 0, 0)),
            pl.BlockSpec((ROPE_DIM // 2, 1), lambda i, j: (0, 0)),
            pl.BlockSpec((1, d), lambda i, j: (0, 0)),
            pl.BlockSpec((1, d), lambda i, j: (0, 0)),
            pl.BlockSpec((d, tn), lambda i, j: (0, jnp.minimum(j, kv_tiles - 1))),
            pl.BlockSpec((d, tn), lambda i, j: (0, jnp.maximum(j - kv_tiles, 0))),
        ],
        out_specs=pl.BlockSpec((tm, tn), lambda i, j: (i, j)),
        out_shape=jax.ShapeDtypeStruct((t, (kv_tiles + q_tiles) * tn), BF16),
        scratch_shapes=[
            pltpu.VMEM((tm, d), BF16),
            pltpu.VMEM((tm, d), BF16),
            pltpu.VMEM((tm, LANES), F32),
            pltpu.VMEM((tm, LANES), F32),
            pltpu.VMEM((tm, LANES), F32),
            pltpu.VMEM((tm, d), F32),
            pltpu.SemaphoreType.DMA,
        ],
        compiler_params=_params(("arbitrary", "arbitrary")),
        name="norm_kvq_rope",
    )(h, pos_rows, inv_freq_col, g_kv, g_q, w_kv, w_q)


def _diff_attn_kernel(k_ref, v_ref, q_ref, lam_ref, g_ref, *rest, lambda_init, seq, n_cast):
    cast_src, o_ref, cast_dst = rest[:n_cast], rest[n_cast], rest[n_cast + 1:]
    tb = ATT_T
    _cast_blocks(cast_src, cast_dst)
    dk = A_QK_DIM
    nblk = seq // tb

    lv = lam_ref[...]
    lam = (jnp.exp(jnp.sum(lv[0:1, :] * lv[1:2, :], axis=1, keepdims=True))
           - jnp.exp(jnp.sum(lv[2:3, :] * lv[3:4, :], axis=1, keepdims=True)) + lambda_init)
    g_col = g_ref[...] * (1.0 - lambda_init)
    key_idx = lax.broadcasted_iota(jnp.int32, (tb, tb), 0)
    qry_idx = lax.broadcasted_iota(jnp.int32, (tb, tb), 1)
    diag_mask = key_idx <= qry_idx
    v_t = [v_ref[kb * tb:(kb + 1) * tb, :].astype(F32).T.astype(BF16) for kb in range(nblk)]

    for qi in range(nblk):
        q = q_ref[qi * tb:(qi + 1) * tb, :]
        comps = []
        for c in range(2):
            qc = q[:, c * dk:(c + 1) * dk]
            m = l = acc = None
            for kb in range(qi + 1):
                kc = k_ref[kb * tb:(kb + 1) * tb, c * dk:(c + 1) * dk]
                s = lax.dot_general(kc, qc, NT_DIMS, preferred_element_type=F32)
                if kb == qi:
                    s = jnp.where(diag_mask, s, -jnp.inf)
                s_max = jnp.max(s, axis=0, keepdims=True)
                if kb == 0:
                    m = s_max
                    p = jnp.exp2(s - m)
                    l = jnp.sum(p, axis=0, keepdims=True)
                    acc = jnp.dot(v_t[kb], p.astype(BF16), preferred_element_type=F32)
                else:
                    m_new = jnp.maximum(m, s_max)
                    alpha = jnp.exp2(m - m_new)
                    p = jnp.exp2(s - m_new)
                    l = alpha * l + jnp.sum(p, axis=0, keepdims=True)
                    acc = alpha * acc + jnp.dot(v_t[kb], p.astype(BF16), preferred_element_type=F32)
                    m = m_new
            comps.append(acc * (1.0 / l))
        o_t = comps[0] - lam * comps[1]
        o_t = o_t * lax.rsqrt(jnp.mean(o_t * o_t, axis=0, keepdims=True) + EPS) * g_col
        o_ref[qi * tb:(qi + 1) * tb, :] = o_t.T.astype(o_ref.dtype)


def _diff_attention(kvq, lam_vecs, g_subln_col, cast_weights, batch, seq, lambda_init):
    hv = A_V_DIM
    cast_in, cast_out, cast_shapes = _cast_specs(cast_weights, batch * A_HEADS, lambda b, h: b * A_HEADS + h)
    return pl.pallas_call(
        functools.partial(_diff_attn_kernel, lambda_init=lambda_init, seq=seq, n_cast=len(cast_weights)),
        grid=(batch, A_HEADS),
        in_specs=[
            pl.BlockSpec((seq, hv), lambda b, h: (b, h)),
            pl.BlockSpec((seq, hv), lambda b, h: (b, A_HEADS + h)),
            pl.BlockSpec((seq, hv), lambda b, h: (b, 2 * A_HEADS + h)),
            pl.BlockSpec((4, A_QK_DIM), lambda b, h: (0, 0)),
            pl.BlockSpec((hv, 1), lambda b, h: (0, 0)),
        ] + cast_in,
        out_specs=[pl.BlockSpec((seq, hv), lambda b, h: (b, h))] + cast_out,
        out_shape=[jax.ShapeDtypeStruct((batch * seq, A_HEADS * hv), BF16)] + cast_shapes,
        compiler_params=_params(("parallel", "parallel")),
        name="diff_attention",
    )(kvq, kvq, kvq, lam_vecs, g_subln_col, *[w for w, _ in cast_weights])


def kernel(x, positions, a_norm, m_w_in, m_b_igate, m_b_fgate, m_w_hnorm, m_w_out, kv_norm, w_kv, b_norm, w_q, lam_q1, lam_k1, lam_q2, lam_k2, subln, w_o, f_norm, w_up, conv_w, conv_b, w_down, final_norm):
    batch, seq, d = x.shape
    t = batch * seq
    depth = f_norm.shape[0]
    assert depth == 2 and a_norm.shape[0] == 1 and b_norm.shape[0] == 1
    assert seq % FFN_TM == 0 and seq % ATT_T == 0 and t % PROJ_TM == 0
    assert seq % MLSTM_L == 0 and 4 * (seq // MLSTM_L) <= MLSTM_L

    h = x.reshape(t, d)

    qkv_cols = 2 * M_HEADS * M_QK_DIM + M_HEADS * M_V_DIM
    o_cols = M_HEADS * M_V_DIM
    w_in = m_w_in[0].astype(BF16)
    w_gates = jnp.pad(m_w_in[0][:, qkv_cols + o_cols:], ((0, 0), (0, LANES - 2 * M_HEADS))).astype(BF16)
    qkv, o_gate, gates, w_down_0, w_kv_b = _w_in_proj(h, a_norm[0][None, :], w_in, w_gates, qkv_cols, o_cols,
                                                      ((w_down, 0, 44), (w_kv[None], 0, 32)))

    nc = seq // MLSTM_L
    gates = gates[:, :2 * M_HEADS].reshape(batch, nc, MLSTM_L, 2, M_HEADS)
    gates_r = gates.transpose(0, 4, 3, 1, 2)
    bias = jnp.stack([m_b_igate[0], m_b_fgate[0]], axis=1)
    hg, w_up_0, w_out_b, w_q_b = _mlstm(
        qkv, o_gate, gates_r, bias[:, :, None], m_w_hnorm[0][:, None, :],
        ((w_up, 0), (m_w_out, 0), (w_q, 0)), batch, seq)
    h = _matmul_residual(hg, w_out_b, h, RES_TM, RES_TN, "w_out_residual")

    f_gain = f_norm[:, None, :]
    conv_b3 = conv_b[:, None, :]
    h = _conv_ffn(h, 0, f_gain, w_up_0, conv_w, conv_b3, w_down_0, final_norm[None, :], seq, False, "conv_ffn_0")

    layer = 1
    lambda_init = 0.8 - 0.6 * math.exp(-0.3 * layer)
    half = ROPE_DIM // 2
    inv_freq_col = (ROPE_THETA ** (-jnp.arange(half, dtype=F32) / half))[:, None]
    k_cols = A_HEADS * 2 * A_QK_DIM
    pos_rows = positions.reshape(t // PROJ_TM, 1, PROJ_TM)
    kvq = _kvq_proj(h, pos_rows, inv_freq_col, kv_norm[None, :], b_norm[0][None, :],
                    w_kv_b, w_q_b, k_cols)
    lam_vecs = jnp.stack([lam_q1[0], lam_k1[0], lam_q2[0], lam_k2[0]], axis=0)
    att, w_up_1, w_down_1, w_o_b = _diff_attention(kvq, lam_vecs, subln[0][:, None],
                                                   ((w_up, 1), (w_down, 1), (w_o, 0)), batch, seq, lambda_init)
    h = _matmul_residual(att, w_o_b, h, RES_TM, RES_TN, "w_o_residual")

    h = _conv_ffn(h, 1, f_gain, w_up_1, conv_w, conv_b3, w_down_1, final_norm[None, :], seq, True, "conv_ffn_1")
    return h.reshape(batch, seq, d)
```

```python
import functools
import math

import jax
import jax.numpy as jnp
from jax import lax
from jax.experimental import pallas as pl
from jax.experimental.pallas import tpu as pltpu

F32 = jnp.float32
BF16 = jnp.bfloat16

EPS = 1e-6
M_HEADS = 8
M_QK_DIM = 128
M_V_DIM = 256
GATE_SOFTCAP = 15.0
A_HEADS = 8
A_QK_DIM = 128
A_V_DIM = 256
ROPE_DIM = 32
ROPE_THETA = 500000.0
CONV_W = 3

LANES = 128
V7X_VMEM_LIMIT_BYTES = 56 * 1024 * 1024

PROJ_TM = 1024
W_IN_TN = 1024
KVQ_TN = 1024
KVQ_ROWS = 512
RES_TM = 512
RES_TN = 512
FFN_TM = 1024
FFN_ROWS = 512
FFN_TF = 512
FFN_TAIL = 8
ATT_T = 256
MLSTM_L = LANES
MLSTM_HEADS = 2

NT_DIMS = (((1,), (1,)), ((), ()))
Q_SCALE_LOG2 = (A_QK_DIM ** -0.5) * math.log2(math.e)


def _rms_scale(x):
    return x * lax.rsqrt(jnp.mean(x * x, axis=-1, keepdims=True) + EPS)


def _params(semantics):
    return pltpu.CompilerParams(dimension_semantics=semantics, vmem_limit_bytes=V7X_VMEM_LIMIT_BYTES)


def _cast_specs(weights, steps, step_of):
    in_specs, out_specs, out_shapes = [], [], []
    for w, layer in weights:
        rows, cols = w.shape[1:]
        assert rows % steps == 0
        blk = rows // steps
        in_specs.append(pl.BlockSpec((None, blk, cols), lambda *idx, layer=layer: (layer, step_of(*idx), 0)))
        out_specs.append(pl.BlockSpec((blk, cols), lambda *idx: (step_of(*idx), 0)))
        out_shapes.append(jax.ShapeDtypeStruct((rows, cols), BF16))
    return in_specs, out_specs, out_shapes


def _cast_blocks(src_refs, dst_refs):
    for src, dst in zip(src_refs, dst_refs):
        dst[...] = src[...].astype(BF16)


def _tile_copy(x_hbm, x_buf, sem, tile):
    tm = x_buf.shape[0]
    return pltpu.make_async_copy(x_hbm.at[pl.ds(pl.multiple_of(tile * tm, tm), tm), :], x_buf, sem)


def _first_tile(x_hbm, x_buf, sem):
    @pl.when((pl.program_id(0) == 0) & (pl.program_id(1) == 0))
    def _():
        _tile_copy(x_hbm, x_buf, sem, 0).start()
        _tile_copy(x_hbm, x_buf, sem, 0).wait()


def _next_tile(i):
    return jnp.minimum(i + 1, pl.num_programs(0) - 1)


def _next_tile_wait(x_hbm, x_buf, sem):
    @pl.when(pl.program_id(1) == pl.num_programs(1) - 1)
    def _():
        _tile_copy(x_hbm, x_buf, sem, _next_tile(pl.program_id(0))).wait()


def _w_in_kernel(x_hbm, g_ref, w_ref, wg_ref, *rest, qkv_tiles, n_cast):
    cast_src, (qkv_ref, o_ref, gates_ref) = rest[:n_cast], rest[n_cast:n_cast + 3]
    cast_dst, (xn_ref, x_buf, sem) = rest[n_cast + 3:2 * n_cast + 3], rest[2 * n_cast + 3:]
    i = pl.program_id(0)
    j = pl.program_id(1)
    _first_tile(x_hbm, x_buf, sem)

    @pl.when(j == 0)
    def _():
        xn = (_rms_scale(x_buf[...]) * g_ref[...]).astype(BF16)
        xn_ref[...] = xn
        _tile_copy(x_hbm, x_buf, sem, _next_tile(i)).start()
        gates_ref[...] = jnp.dot(xn, wg_ref[...], preferred_element_type=F32)
        y = jnp.dot(xn, w_ref[...], preferred_element_type=F32)
        qkv_ref[...] = (y * (M_QK_DIM ** -0.5)).astype(BF16)
        _cast_blocks(cast_src, cast_dst)

    @pl.when((j > 0) & (j < qkv_tiles))
    def _():
        qkv_ref[...] = jnp.dot(xn_ref[...], w_ref[...], preferred_element_type=F32).astype(BF16)
        _cast_blocks(cast_src, cast_dst)

    @pl.when(j >= qkv_tiles)
    def _():
        o_ref[...] = jnp.dot(xn_ref[...], w_ref[...], preferred_element_type=F32)
        _cast_blocks(cast_src, cast_dst)

    _next_tile_wait(x_hbm, x_buf, sem)


def _w_in_proj(x, g, w, w_gates, qkv_cols, o_cols, cast_weights):
    t, d = x.shape
    tm, tn = PROJ_TM, W_IN_TN
    assert M_HEADS * M_QK_DIM == tn and qkv_cols % tn == 0 and o_cols % tn == 0
    qkv_tiles = qkv_cols // tn
    n_j = (qkv_cols + o_cols) // tn
    cast_in, cast_out, cast_shapes = [], [], []
    for w_cast, layer, steps in cast_weights:
        assert steps <= (t // tm) * n_j
        specs = _cast_specs(((w_cast, layer),), steps, lambda i, j, steps=steps: jnp.minimum(i * n_j + j, steps - 1))
        cast_in, cast_out, cast_shapes = cast_in + specs[0], cast_out + specs[1], cast_shapes + specs[2]
    return pl.pallas_call(
        functools.partial(_w_in_kernel, qkv_tiles=qkv_tiles, n_cast=len(cast_weights)),
        grid=(t // tm, n_j),
        in_specs=[
            pl.BlockSpec(memory_space=pl.ANY),
            pl.BlockSpec((1, d), lambda i, j: (0, 0)),
            pl.BlockSpec((d, tn), lambda i, j: (0, j)),
            pl.BlockSpec((d, LANES), lambda i, j: (0, 0)),
        ] + cast_in,
        out_specs=[
            pl.BlockSpec((tm, tn), lambda i, j: (i, jnp.minimum(j, qkv_tiles - 1))),
            pl.BlockSpec((tm, tn), lambda i, j: (i, jnp.maximum(j - qkv_tiles, 0))),
            pl.BlockSpec((tm, LANES), lambda i, j: (i, 0)),
        ] + cast_out,
        out_shape=[
            jax.ShapeDtypeStruct((t, qkv_cols), BF16),
            jax.ShapeDtypeStruct((t, o_cols), F32),
            jax.ShapeDtypeStruct((t, LANES), F32),
        ] + cast_shapes,
        scratch_shapes=[pltpu.VMEM((tm, d), BF16), pltpu.VMEM((tm, d), F32), pltpu.SemaphoreType.DMA],
        compiler_params=_params(("arbitrary", "arbitrary")),
        name="norm_w_in",
    )(x, g, w, w_gates, *[w_cast for w_cast, _, _ in cast_weights])


def _softcap(t):
    return GATE_SOFTCAP * jnp.tanh(t / GATE_SOFTCAP)


def _log_sigmoid(x):
    return jnp.minimum(x, 0.0) - jnp.log1p(jnp.exp(-jnp.abs(x)))


def _lane_scan(x, op, fill):
    lane = lax.broadcasted_iota(jnp.int32, x.shape, 1)
    d = 1
    while d < x.shape[1]:
        x = op(x, jnp.where(lane >= d, pltpu.roll(x, d, 1), fill))
        d *= 2
    return x


def _mlstm_gate_tables(gr, bias, nc):
    L = MLSTM_L
    i_rows = _softcap(gr[0] + bias[0:1, :])
    f_rows = _log_sigmoid(_softcap(gr[1] + bias[1:2, :]))
    b_rows = _lane_scan(f_rows, jnp.add, 0.0)
    a_rows = i_rows - b_rows
    pm_rows = _lane_scan(a_rows, jnp.maximum, -jnp.inf)
    b_last = b_rows[:, L - 1:L]
    a_max = pm_rows[:, L - 1:L]

    m = jnp.zeros((1, 1), F32)
    ms = [m]
    for c in range(nc):
        m = b_last[c:c + 1, :] + jnp.maximum(m, a_max[c:c + 1, :])
        ms.append(m)
    m_in = jnp.concatenate(ms[:nc], axis=0)
    m_out = jnp.concatenate(ms[1:], axis=0)
    big_m_rows = jnp.maximum(m_in, pm_rows)
    decay = jnp.exp(b_last + m_in - m_out)

    stacked = jnp.concatenate([
        big_m_rows,
        jnp.exp(m_in - big_m_rows),
        jnp.exp(-(b_rows + big_m_rows)),
        jnp.exp(a_rows + b_last - m_out),
        jnp.zeros((L - 4 * nc, L), F32)], axis=0)
    return a_rows, stacked.T, decay


def _mlstm_kernel(q_ref, k_ref, v_ref, o_ref, gr_ref, br_ref, wn_ref, *rest, seq, n_cast):
    cast_src, out_ref, cast_dst = rest[:n_cast], rest[n_cast], rest[n_cast + 1:]
    L = MLSTM_L
    nc = seq // L
    dk, dv = M_QK_DIM, M_V_DIM
    heads = range(MLSTM_HEADS)
    _cast_blocks(cast_src, cast_dst)
    tables = [_mlstm_gate_tables(gr_ref[hh], br_ref[hh], nc) for hh in heads]

    row_t = lax.broadcasted_iota(jnp.int32, (L, L), 0)
    col_s = lax.broadcasted_iota(jnp.int32, (L, L), 1)
    causal = row_t >= col_s
    ones_blk = jnp.ones((L, LANES), BF16)

    states = [jnp.zeros((dk, dv + LANES), F32) for _ in heads]
    for c in range(nc):
        rows = slice(c * L, (c + 1) * L)
        for hh in heads:
            a_rows, cols, decay = tables[hh]
            qb = q_ref[rows, hh * dk:(hh + 1) * dk]
            kb = k_ref[rows, hh * dk:(hh + 1) * dk]
            v_aug = jnp.concatenate([v_ref[rows, hh * dv:(hh + 1) * dv], ones_blk], axis=1)
            big_m = cols[:, c:c + 1]
            w_inter = cols[:, nc + c:nc + c + 1]
            e_neg_m = cols[:, 2 * nc + c:2 * nc + c + 1]
            w_key = cols[:, 3 * nc + c:3 * nc + c + 1]

            e = jnp.exp(jnp.where(causal, a_rows[c:c + 1, :] - big_m, -jnp.inf))
            s = lax.dot_general(qb, kb, NT_DIMS, preferred_element_type=F32) * e
            nd = (w_inter * jnp.dot(qb, states[hh].astype(BF16), preferred_element_type=F32)
                  + jnp.dot(s.astype(BF16), v_aug, preferred_element_type=F32))
            den = nd[:, dv:]
            r = 1.0 / jnp.maximum(jnp.abs(den), e_neg_m)
            h = nd[:, :dv] * jnp.concatenate([r] * (dv // LANES), axis=1)

            hn = _rms_scale(h) * wn_ref[hh]
            gate = jax.nn.sigmoid(o_ref[rows, hh * dv:(hh + 1) * dv])
            out_ref[rows, hh * dv:(hh + 1) * dv] = (hn * gate).astype(out_ref.dtype)

            kw_t = (kb.astype(F32) * w_key).T.astype(BF16)
            states[hh] = decay[c:c + 1, :] * states[hh] + jnp.dot(kw_t, v_aug, preferred_element_type=F32)


def _mlstm(qkv, o_gate, gates_r, bias_r, w_hnorm, cast_weights, batch, seq):
    nc = seq // MLSTM_L
    g = MLSTM_HEADS
    groups = M_HEADS // g
    cast_in, cast_out, cast_shapes = _cast_specs(cast_weights, batch * groups, lambda b, h: b * groups + h)
    qk_w, v_w = g * M_QK_DIM, g * M_V_DIM
    k_off = M_HEADS * M_QK_DIM // qk_w
    v_off = 2 * M_HEADS * M_QK_DIM // v_w
    return pl.pallas_call(
        functools.partial(_mlstm_kernel, seq=seq, n_cast=len(cast_weights)),
        grid=(batch, groups),
        in_specs=[
            pl.BlockSpec((seq, qk_w), lambda b, h: (b, h)),
            pl.BlockSpec((seq, qk_w), lambda b, h: (b, k_off + h)),
            pl.BlockSpec((seq, v_w), lambda b, h: (b, v_off + h)),
            pl.BlockSpec((seq, v_w), lambda b, h: (b, h)),
            pl.BlockSpec((None, g, 2, nc, MLSTM_L), lambda b, h: (b, h, 0, 0, 0)),
            pl.BlockSpec((g, 2, 1), lambda b, h: (h, 0, 0)),
            pl.BlockSpec((g, 1, M_V_DIM), lambda b, h: (h, 0, 0)),
        ] + cast_in,
        out_specs=[pl.BlockSpec((seq, v_w), lambda b, h: (b, h))] + cast_out,
        out_shape=[jax.ShapeDtypeStruct((batch * seq, M_HEADS * M_V_DIM), BF16)] + cast_shapes,
        compiler_params=_params(("parallel", "parallel")),
        name="mlstm_chunkwise",
    )(qkv, qkv, qkv, o_gate, gates_r, bias_r, w_hnorm, *[w for w, _ in cast_weights])


def _matmul_residual_kernel(a_ref, w_ref, r_ref, o_ref, *, tn):
    a = a_ref[...]
    for j in range(w_ref.shape[1] // tn):
        cols = slice(j * tn, (j + 1) * tn)
        o_ref[:, cols] = r_ref[:, cols] + jnp.dot(a, w_ref[:, cols], preferred_element_type=F32)


def _matmul_residual(a, w, res, tm, tn, name):
    t, kdim = a.shape
    n = w.shape[1]
    return pl.pallas_call(
        functools.partial(_matmul_residual_kernel, tn=tn),
        grid=(t // tm,),
        in_specs=[
            pl.BlockSpec((tm, kdim), lambda i: (i, 0)),
            pl.BlockSpec((kdim, n), lambda i: (0, 0)),
            pl.BlockSpec((tm, n), lambda i: (i, 0)),
        ],
        out_specs=pl.BlockSpec((tm, n), lambda i: (i, 0)),
        out_shape=jax.ShapeDtypeStruct((t, n), F32),
        compiler_params=_params(("parallel",)),
        name=name,
    )(a, w, res)


def _conv_ffn_kernel(h_ref, g_ref, wg_ref, wv_ref, cwg_ref, cwv_ref, cbg_ref, cbv_ref, wd_ref,
                     gf_ref, o_ref, xn_ref, tail_ref, u_ref, *, tiles_per_seq, final_norm):
    i = pl.program_id(0)
    f = pl.program_id(1)
    tm = h_ref.shape[0]

    @pl.when(i % tiles_per_seq == 0)
    def _():
        tail_ref[f] = jnp.zeros(tail_ref.shape[1:], F32)

    def step(first):
        if first:
            xn_ref[...] = (_rms_scale(h_ref[...]) * g_ref[...]).astype(BF16)
        history = [tail_ref[f, 0], tail_ref[f, 1]]
        for r in range(tm // FFN_ROWS):
            rows = slice(r * FFN_ROWS, (r + 1) * FFN_ROWS)
            xn = xn_ref[rows, :]

            def conv(w_ref, cw_ref, cb_ref, slot):
                u_ref[slot, :FFN_TAIL, :] = history[slot]
                u_ref[slot, FFN_TAIL:, :] = jnp.dot(xn, w_ref[...], preferred_element_type=F32)
                ue = u_ref[slot]
                history[slot] = ue[FFN_ROWS:, :]
                cw = cw_ref[...]
                c = (cb_ref[...] + pltpu.roll(ue, 2, 0) * cw[0:1, :] + pltpu.roll(ue, 1, 0) * cw[1:2, :]
                     + ue * cw[2:3, :])
                return c[FFN_TAIL:, :]

            gate = conv(wg_ref, cwg_ref, cbg_ref, 0)
            val = conv(wv_ref, cwv_ref, cbv_ref, 1)
            act = (gate * jax.nn.sigmoid(gate) * val).astype(BF16)
            base = h_ref[rows, :] if first else o_ref[rows, :]
            o_ref[rows, :] = base + jnp.dot(act, wd_ref[...], preferred_element_type=F32)
        tail_ref[f, 0] = history[0]
        tail_ref[f, 1] = history[1]

    @pl.when(f == 0)
    def _():
        step(True)

    @pl.when(f > 0)
    def _():
        step(False)

    if final_norm:
        @pl.when(f == pl.num_programs(1) - 1)
        def _():
            o_ref[...] = _rms_scale(o_ref[...]) * gf_ref[...]


def _conv_ffn(h, layer, g, w_up, conv_w, conv_b, w_down, g_final, seq, final_norm, name):
    t, d = h.shape
    d_ff = w_down.shape[0]
    tm, tf = FFN_TM, FFN_TF
    nf = d_ff // tf
    kern = functools.partial(_conv_ffn_kernel, tiles_per_seq=seq // tm, final_norm=final_norm)
    return pl.pallas_call(
        kern,
        grid=(t // tm, nf),
        in_specs=[
            pl.BlockSpec((tm, d), lambda i, f: (i, 0)),
            pl.BlockSpec((None, 1, d), lambda i, f: (layer, 0, 0)),
            pl.BlockSpec((d, tf), lambda i, f: (0, f)),
            pl.BlockSpec((d, tf), lambda i, f: (0, nf + f)),
            pl.BlockSpec((None, CONV_W, tf), lambda i, f: (layer, 0, f)),
            pl.BlockSpec((None, CONV_W, tf), lambda i, f: (layer, 0, nf + f)),
            pl.BlockSpec((None, 1, tf), lambda i, f: (layer, 0, f)),
            pl.BlockSpec((None, 1, tf), lambda i, f: (layer, 0, nf + f)),
            pl.BlockSpec((tf, d), lambda i, f: (f, 0)),
            pl.BlockSpec((1, d), lambda i, f: (0, 0)),
        ],
        out_specs=pl.BlockSpec((tm, d), lambda i, f: (i, 0)),
        out_shape=jax.ShapeDtypeStruct((t, d), F32),
        scratch_shapes=[
            pltpu.VMEM((tm, d), BF16),
            pltpu.VMEM((nf, 2, FFN_TAIL, tf), F32),
            pltpu.VMEM((2, FFN_TAIL + FFN_ROWS, tf), F32),
        ],
        compiler_params=_params(("arbitrary", "arbitrary")),
        name=name,
    )(h, g, w_up, w_up, conv_w, conv_w, conv_b, conv_b, w_down, g_final)


def _rope(y, cos_t, sin_lo, sin_hi):
    outs = []
    for gidx in range(y.shape[1] // LANES):
        blk = y[:, gidx * LANES:(gidx + 1) * LANES]
        half = ROPE_DIM // 2
        outs.append(blk * cos_t + pltpu.roll(blk, half, 1) * sin_hi + pltpu.roll(blk, LANES - half, 1) * sin_lo)
    return jnp.concatenate(outs, axis=1)


def _kvq_kernel(x_hbm, pos_ref, invf_ref, gkv_ref, gq_ref, wkv_ref, wq_ref, o_ref,
                xkv_ref, xq_ref, cos_ref, slo_ref, shi_ref, x_buf, sem, *, k_tiles, kv_tiles):
    i = pl.program_id(0)
    j = pl.program_id(1)
    _first_tile(x_hbm, x_buf, sem)

    def prologue():
        y = _rms_scale(x_buf[...])
        xkv_ref[...] = (y * gkv_ref[...]).astype(BF16)
        xq_ref[...] = (y * gq_ref[...]).astype(BF16)
        _tile_copy(x_hbm, x_buf, sem, _next_tile(i)).start()
        half = ROPE_DIM // 2
        ang = invf_ref[...] * pos_ref[...].astype(F32)
        cos = jnp.cos(ang)
        sin = jnp.sin(ang)
        pad = LANES - ROPE_DIM
        tm = ang.shape[1]
        cos_ref[...] = jnp.concatenate([cos, cos, jnp.ones((pad, tm), F32)], axis=0).T
        sin_t = jnp.concatenate([-sin, sin, jnp.zeros((pad, tm), F32)], axis=0).T
        lane = lax.broadcasted_iota(jnp.int32, sin_t.shape, 1)
        slo_ref[...] = jnp.where(lane < half, sin_t, 0.0)
        shi_ref[...] = jnp.where(lane >= half, sin_t, 0.0)

    halves = [slice(r * KVQ_ROWS, (r + 1) * KVQ_ROWS) for r in range(x_buf.shape[0] // KVQ_ROWS)]

    def rotated(xn_ref, w_ref, rows, scale):
        y = jnp.dot(xn_ref[rows, :], w_ref[...], preferred_element_type=F32)
        y = _rope(y, cos_ref[rows, :], slo_ref[rows, :], shi_ref[rows, :])
        return y if scale is None else y * scale

    def k_step():
        for rows in halves:
            o_ref[rows, :] = rotated(xkv_ref, wkv_ref, rows, None).astype(BF16)

    @pl.when(j == 0)
    def _():
        prologue()
        k_step()

    @pl.when((j > 0) & (j < k_tiles))
    def _():
        k_step()

    @pl.when((j >= k_tiles) & (j < kv_tiles))
    def _():
        o_ref[...] = jnp.dot(xkv_ref[...], wkv_ref[...], preferred_element_type=F32).astype(BF16)

    @pl.when(j >= kv_tiles)
    def _():
        for rows in halves:
            o_ref[rows, :] = rotated(xq_ref, wq_ref, rows, Q_SCALE_LOG2).astype(BF16)

    _next_tile_wait(x_hbm, x_buf, sem)


def _kvq_proj(h, pos_rows, inv_freq_col, g_kv, g_q, w_kv, w_q, k_cols):
    t, d = h.shape
    tm, tn = PROJ_TM, KVQ_TN
    k_tiles = k_cols // tn
    kv_tiles = w_kv.shape[1] // tn
    q_tiles = w_q.shape[1] // tn
    kern = functools.partial(_kvq_kernel, k_tiles=k_tiles, kv_tiles=kv_tiles)
    return pl.pallas_call(
        kern,
        grid=(t // tm, kv_tiles + q_tiles),
        in_specs=[
            pl.BlockSpec(memory_space=pl.ANY),
            pl.BlockSpec((None, 1, tm), lambda i, j: (i, 0, 0)),
            pl.BlockSpec((ROPE_DIM // 2, 1), lambda i, j: (0, 0)),
            pl.BlockSpec((1, d), lambda i, j: (0, 0)),
            pl.BlockSpec((1, d), lambda i, j: (0, 0)),
            pl.BlockSpec((d, tn), lambda i, j: (0, jnp.minimum(j, kv_tiles - 1))),
            pl.BlockSpec((d, tn), lambda i, j: (0, jnp.maximum(j - kv_tiles, 0))),
        ],
        out_specs=pl.BlockSpec((tm, tn), lambda i, j: (i, j)),
        out_shape=jax.ShapeDtypeStruct((t, (kv_tiles + q_tiles) * tn), BF16),
        scratch_shapes=[
            pltpu.VMEM((tm, d), BF16),
            pltpu.VMEM((tm, d), BF16),
            pltpu.VMEM((tm, LANES), F32),
            pltpu.VMEM((tm, LANES), F32),
            pltpu.VMEM((tm, LANES), F32),
            pltpu.VMEM((tm, d), F32),
            pltpu.SemaphoreType.DMA,
        ],
        compiler_params=_params(("arbitrary", "arbitrary")),
        name="norm_kvq_rope",
    )(h, pos_rows, inv_freq_col, g_kv, g_q, w_kv, w_q)


def _diff_attn_kernel(k_ref, v_ref, q_ref, lam_ref, g_ref, *rest, lambda_init, seq, n_cast):
    cast_src, o_ref, cast_dst = rest[:n_cast], rest[n_cast], rest[n_cast + 1:]
    tb = ATT_T
    _cast_blocks(cast_src, cast_dst)
    dk = A_QK_DIM
    nblk = seq // tb

    lv = lam_ref[...]
    lam = (jnp.exp(jnp.sum(lv[0:1, :] * lv[1:2, :], axis=1, keepdims=True))
           - jnp.exp(jnp.sum(lv[2:3, :] * lv[3:4, :], axis=1, keepdims=True)) + lambda_init)
    g_col = g_ref[...] * (1.0 - lambda_init)
    key_idx = lax.broadcasted_iota(jnp.int32, (tb, tb), 0)
    qry_idx = lax.broadcasted_iota(jnp.int32, (tb, tb), 1)
    diag_mask = key_idx <= qry_idx
    v_t = [v_ref[kb * tb:(kb + 1) * tb, :].astype(F32).T.astype(BF16) for kb in range(nblk)]

    for qi in range(nblk):
        q = q_ref[qi * tb:(qi + 1) * tb, :]
        comps = []
        for c in range(2):
            qc = q[:, c * dk:(c + 1) * dk]
            m = l = acc = None
            for kb in range(qi + 1):
                kc = k_ref[kb * tb:(kb + 1) * tb, c * dk:(c + 1) * dk]
                s = lax.dot_general(kc, qc, NT_DIMS, preferred_element_type=F32)
                if kb == qi:
                    s = jnp.where(diag_mask, s, -jnp.inf)
                s_max = jnp.max(s, axis=0, keepdims=True)
                if kb == 0:
                    m = s_max
                    p = jnp.exp2(s - m)
                    l = jnp.sum(p, axis=0, keepdims=True)
                    acc = jnp.dot(v_t[kb], p.astype(BF16), preferred_element_type=F32)
                else:
                    m_new = jnp.maximum(m, s_max)
                    alpha = jnp.exp2(m - m_new)
                    p = jnp.exp2(s - m_new)
                    l = alpha * l + jnp.sum(p, axis=0, keepdims=True)
                    acc = alpha * acc + jnp.dot(v_t[kb], p.astype(BF16), preferred_element_type=F32)
                    m = m_new
            comps.append(acc * (1.0 / l))
        o_t = comps[0] - lam * comps[1]
        o_t = o_t * lax.rsqrt(jnp.mean(o_t * o_t, axis=0, keepdims=True) + EPS) * g_col
        o_ref[qi * tb:(qi + 1) * tb, :] = o_t.T.astype(o_ref.dtype)


def _diff_attention(kvq, lam_vecs, g_subln_col, cast_weights, batch, seq, lambda_init):
    hv = A_V_DIM
    cast_in, cast_out, cast_shapes = _cast_specs(cast_weights, batch * A_HEADS, lambda b, h: b * A_HEADS + h)
    return pl.pallas_call(
        functools.partial(_diff_attn_kernel, lambda_init=lambda_init, seq=seq, n_cast=len(cast_weights)),
        grid=(batch, A_HEADS),
        in_specs=[
            pl.BlockSpec((seq, hv), lambda b, h: (b, h)),
            pl.BlockSpec((seq, hv), lambda b, h: (b, A_HEADS + h)),
            pl.BlockSpec((seq, hv), lambda b, h: (b, 2 * A_HEADS + h)),
            pl.BlockSpec((4, A_QK_DIM), lambda b, h: (0, 0)),
            pl.BlockSpec((hv, 1), lambda b, h: (0, 0)),
        ] + cast_in,
        out_specs=[pl.BlockSpec((seq, hv), lambda b, h: (b, h))] + cast_out,
        out_shape=[jax.ShapeDtypeStruct((batch * seq, A_HEADS * hv), BF16)] + cast_shapes,
        compiler_params=_params(("parallel", "parallel")),
        name="diff_attention",
    )(kvq, kvq, kvq, lam_vecs, g_subln_col, *[w for w, _ in cast_weights])


def kernel(x, positions, a_norm, m_w_in, m_b_igate, m_b_fgate, m_w_hnorm, m_w_out, kv_norm, w_kv, b_norm, w_q, lam_q1, lam_k1, lam_q2, lam_k2, subln, w_o, f_norm, w_up, conv_w, conv_b, w_down, final_norm):
    batch, seq, d = x.shape
    t = batch * seq
    depth = f_norm.shape[0]
    assert depth == 2 and a_norm.shape[0] == 1 and b_norm.shape[0] == 1
    assert seq % FFN_TM == 0 and seq % ATT_T == 0 and t % PROJ_TM == 0
    assert seq % MLSTM_L == 0 and 4 * (seq // MLSTM_L) <= MLSTM_L

    h = x.reshape(t, d)

    qkv_cols = 2 * M_HEADS * M_QK_DIM + M_HEADS * M_V_DIM
    o_cols = M_HEADS * M_V_DIM
    w_in = m_w_in[0].astype(BF16)
    w_gates = jnp.pad(m_w_in[0][:, qkv_cols + o_cols:], ((0, 0), (0, LANES - 2 * M_HEADS))).astype(BF16)
    qkv, o_gate, gates, w_down_0, w_kv_b, w_out_b, w_q_b = _w_in_proj(
        h, a_norm[0][None, :], w_in, w_gates, qkv_cols, o_cols,
        ((w_down, 0, 44), (w_kv[None], 0, 32), (m_w_out, 0, 32), (w_q, 0, 32)))

    nc = seq // MLSTM_L
    gates = gates[:, :2 * M_HEADS].reshape(batch, nc, MLSTM_L, 2, M_HEADS)
    gates_r = gates.transpose(0, 4, 3, 1, 2)
    bias = jnp.stack([m_b_igate[0], m_b_fgate[0]], axis=1)
    hg, w_up_0 = _mlstm(
        qkv, o_gate, gates_r, bias[:, :, None], m_w_hnorm[0][:, None, :],
        ((w_up, 0),), batch, seq)
    h = _matmul_residual(hg, w_out_b, h, RES_TM, RES_TN, "w_out_residual")

    f_gain = f_norm[:, None, :]
    conv_b3 = conv_b[:, None, :]
    h = _conv_ffn(h, 0, f_gain, w_up_0, conv_w, conv_b3, w_down_0, final_norm[None, :], seq, False, "conv_ffn_0")

    layer = 1
    lambda_init = 0.8 - 0.6 * math.exp(-0.3 * layer)
    half = ROPE_DIM // 2
    inv_freq_col = (ROPE_THETA ** (-jnp.arange(half, dtype=F32) / half))[:, None]
    k_cols = A_HEADS * 2 * A_QK_DIM
    pos_rows = positions.reshape(t // PROJ_TM, 1, PROJ_TM)
    kvq = _kvq_proj(h, pos_rows, inv_freq_col, kv_norm[None, :], b_norm[0][None, :],
                    w_kv_b, w_q_b, k_cols)
    lam_vecs = jnp.stack([lam_q1[0], lam_k1[0], lam_q2[0], lam_k2[0]], axis=0)
    att, w_up_1, w_down_1, w_o_b = _diff_attention(kvq, lam_vecs, subln[0][:, None],
                                                   ((w_up, 1), (w_down, 1), (w_o, 0)), batch, seq, lambda_init)
    h = _matmul_residual(att, w_o_b, h, RES_TM, RES_TN, "w_o_residual")

    h = _conv_ffn(h, 1, f_gain, w_up_1, conv_w, conv_b3, w_down_1, final_norm[None, :], seq, True, "conv_ffn_1")
    return h.reshape(batch, seq, d)
```
